```python
import math
import jax, jax.numpy as jnp
from jax import lax
import numpy as np

D_MODEL = 1024
BATCH = 16
SEQ = 2048
DEPTH = 2
DEC_BATCH = 32
DEC_SEQ = 8
PAST_LEN = 16384
PAGE_SIZE = 128

HEAD_DIM = 64
HG_HEADS = 4
HG_DK = 64
HG_DV = 64
ATT_HEADS = 6
GLA_HEADS = 6
GLA_DK = 32
GLA_DV = 64
GLA_GATE_RANK = 16
GLA_TAU = 16.0
MIX_WIDTH = HG_HEADS * HG_DV + ATT_HEADS * HEAD_DIM + GLA_HEADS * GLA_DV
D_FF = 4 * D_MODEL
D_PLE = 256
ROPE_THETA = 500000.0
ROT_FRACTION = 4
DILATED_PATTERNS = ((128, 1), (512, 4), (2048, 16))
MAX_WINDOW = 2048
QUERY_BLOCK = 128
RECUR_CHUNK = 64
EPS = 1e-6
IN_SPLITS = (HG_HEADS * HG_DK, HG_HEADS * HG_DK, HG_HEADS * HG_DV, HG_HEADS * HG_DV,
             ATT_HEADS * HEAD_DIM, ATT_HEADS * HEAD_DIM, ATT_HEADS * HEAD_DIM,
             GLA_HEADS * GLA_DK, GLA_HEADS * GLA_DK, GLA_HEADS * GLA_DV, GLA_GATE_RANK, GLA_HEADS * GLA_DV)
D_IN = sum(IN_SPLITS)

kernel_name = 'hybrid_hgrn2_dilated_gla_decode_step'


def rms_norm(x, w):
    x32 = x.astype(jnp.float32)
    y = x32 * lax.rsqrt(jnp.mean(x32 * x32, axis=-1, keepdims=True) + EPS)
    return (y * w.astype(jnp.float32)).astype(x.dtype)


def partial_rotary(x, pos):
    rd = x.shape[-1] // ROT_FRACTION
    half = rd // 2
    inv = jnp.exp(-math.log(ROPE_THETA) * jnp.arange(half, dtype=jnp.float32) * (2.0 / rd))
    ang = pos[:, None] * inv[None, :]
    cos = jnp.cos(ang)[None, :, None, :]
    sin = jnp.sin(ang)[None, :, None, :]
    x32 = x.astype(jnp.float32)
    x1 = x32[..., :half]
    x2 = x32[..., half:rd]
    out = jnp.concatenate([x1 * cos - x2 * sin, x2 * cos + x1 * sin, x32[..., rd:]], axis=-1)
    return out.astype(x.dtype)


def gated_linear_recurrence(q, k, v, log_f, s0):
    B, T, H, Dk = q.shape
    Dv = v.shape[-1]
    c = min(RECUR_CHUNK, T)
    n = -(-T // c)
    pad = n * c - T

    def prep(a):
        a = jnp.pad(a.astype(jnp.float32), ((0, 0), (0, pad), (0, 0), (0, 0)))
        return jnp.moveaxis(a.reshape(B, n, c, H, a.shape[-1]), 1, 0)

    qs, ks, vs, gs = prep(q), prep(k), prep(v), prep(log_f)
    causal = jnp.tril(jnp.ones((c, c), dtype=bool))[None, :, :, None, None]

    def step(S, blk):
        qc, kc, vc, gc = blk
        b = jnp.cumsum(gc, axis=1)
        o_inter = jnp.einsum('bthk,bhkv->bthv', qc * jnp.exp(b), S)
        diff = b[:, :, None] - b[:, None, :]
        decay = jnp.exp(jnp.where(causal, diff, -jnp.inf))
        A = jnp.einsum('bthk,bshk,btshk->bths', qc, kc, decay)
        o_intra = jnp.einsum('bths,bshv->bthv', A, vc)
        b_last = b[:, -1]
        k_dec = kc * jnp.exp(b_last[:, None] - b)
        S_new = jnp.exp(b_last)[..., None] * S + jnp.einsum('bshk,bshv->bhkv', k_dec, vc)
        return S_new, o_inter + o_intra

    S_T, o = lax.scan(step, s0.astype(jnp.float32), (qs, ks, vs, gs))
    o = jnp.moveaxis(o, 0, 1).reshape(B, n * c, H, Dv)[:, :T]
    return o.astype(v.dtype), S_T.astype(s0.dtype)


def dilated_window_attention(q, k_all, v_all, q_start):
    B, T, H, Dh = q.shape
    bs = QUERY_BLOCK if T % QUERY_BLOCK == 0 else T
    nb = T // bs
    qb = jnp.moveaxis(q.reshape(B, nb, bs, H, Dh), 1, 0)
    scale = Dh ** -0.5

    def block(args):
        bidx, qc = args
        iq = q_start + bidx * bs + jnp.arange(bs)
        q32 = qc.astype(jnp.float32) * scale
        outs, lses = [], []
        for (w, d) in DILATED_PATTERNS:
            n_keys = w // d + 1
            kidx = iq[:, None] - d * jnp.arange(n_keys)[None, :]
            valid = kidx >= 0
            kidx = jnp.maximum(kidx, 0)
            kg = k_all[:, kidx].astype(jnp.float32)
            vg = v_all[:, kidx].astype(jnp.float32)
            s = jnp.einsum('bqhd,bqnhd->bhqn', q32, kg)
            s = jnp.where(valid[None, None], s, -jnp.inf)
            lse = jax.nn.logsumexp(s, axis=-1)
            p = jnp.exp(s - lse[..., None])
            outs.append(jnp.einsum('bhqn,bqnhd->bqhd', p, vg))
            lses.append(lse)
        wts = jax.nn.softmax(jnp.stack(lses, axis=0), axis=0)
        wts = jnp.transpose(wts, (0, 1, 3, 2))[..., None]
        return jnp.sum(wts * jnp.stack(outs, axis=0), axis=0).astype(q.dtype)

    o = lax.map(block, (jnp.arange(nb), qb))
    return jnp.moveaxis(o, 0, 1).reshape(B, T, H, Dh)


def trunk_layer(h, p_i, s_hgrn, s_gla, k_past, v_past, pos_offset, lb,
                norm_pre_mix, w_in, hgrn_norm, attn_norm, gla_w_gate2, gla_b_gate, gla_norm, w_out,
                norm_post_mix, norm_pre_mlp, w_up, w_down, norm_post_mlp, w_ple_gate, w_ple):
    B, T, _ = h.shape
    xn = rms_norm(h, norm_pre_mix)
    proj = xn @ w_in
    split_points = [int(v) for v in np.cumsum(IN_SPLITS)[:-1]]
    hq, hf, hi, hg, aq, ak, av, gq, gk, gv, g_lr, gr = jnp.split(proj, split_points, axis=-1)

    def heads(a, n):
        return a.reshape(B, T, n, -1)

    lb_h = lb.reshape(HG_HEADS, HG_DK)
    f = lb_h + (1.0 - lb_h) * jax.nn.sigmoid(heads(hf, HG_HEADS).astype(jnp.float32))
    o_h, s_hgrn_new = gated_linear_recurrence(heads(hq, HG_HEADS), 1.0 - f, heads(hi, HG_HEADS), jnp.log(f), s_hgrn)
    o_h = rms_norm(o_h, hgrn_norm.reshape(HG_HEADS, HG_DV)) * jax.nn.silu(heads(hg, HG_HEADS))

    pos = jnp.arange(T, dtype=jnp.float32) + pos_offset
    qa = partial_rotary(heads(aq, ATT_HEADS), pos)
    ka = partial_rotary(heads(ak, ATT_HEADS), pos)
    k_all = jnp.concatenate([k_past, ka.astype(k_past.dtype)], axis=1)
    v_all = jnp.concatenate([v_past, heads(av, ATT_HEADS).astype(v_past.dtype)], axis=1)
    o_a = dilated_window_attention(qa, k_all, v_all, k_past.shape[1]).reshape(B, T, -1)
    o_a = rms_norm(o_a, attn_norm)
    L = k_all.shape[1]
    new_len = min(MAX_WINDOW, L)
    k_buf = k_all[:, L - new_len:]
    v_buf = v_all[:, L - new_len:]

    log_a = jax.nn.log_sigmoid((g_lr @ gla_w_gate2 + gla_b_gate).astype(jnp.float32)) / GLA_TAU
    o_g, s_gla_new = gated_linear_recurrence(heads(gq, GLA_HEADS) * (GLA_DK ** -0.5), heads(gk, GLA_HEADS),
                                             heads(gv, GLA_HEADS), heads(log_a, GLA_HEADS), s_gla)
    o_g = rms_norm(o_g, gla_norm.reshape(GLA_HEADS, GLA_DV)) * jax.nn.silu(heads(gr, GLA_HEADS))

    mix = jnp.concatenate([o_h.reshape(B, T, -1), o_a, o_g.reshape(B, T, -1)], axis=-1) @ w_out
    h = h + rms_norm(mix, norm_post_mix)

    u = jnp.square(jax.nn.relu(rms_norm(h, norm_pre_mlp) @ w_up))
    h = h + rms_norm(u @ w_down, norm_post_mlp)

    h = h + jax.nn.sigmoid(h @ w_ple_gate) * (p_i @ w_ple)
    return h, s_hgrn_new, s_gla_new, k_buf, v_buf


def setup_inputs(seed: int = 0) -> dict:
    key = jax.random.key(seed)
    ks = jax.random.split(key, 24)
    f32 = jnp.float32
    win = min(MAX_WINDOW, PAST_LEN)

    def nrm(k, shape, scale):
        return jax.random.normal(k, shape, f32) * scale

    def gain(k, n):
        return 1.0 + 0.1 * jax.random.normal(k, (DEPTH, n), f32)

    return {
        'x_prompt': nrm(ks[0], (BATCH, SEQ, D_MODEL), 1.0),
        'x_sample': nrm(ks[1], (DEC_BATCH, DEC_SEQ, D_MODEL), 1.0),
        'state_hgrn': nrm(ks[2], (DEPTH, DEC_BATCH, HG_HEADS, HG_DK, HG_DV), 0.5),
        'state_gla': nrm(ks[3], (DEPTH, DEC_BATCH, GLA_HEADS, GLA_DK, GLA_DV), 0.5),
        'cache_k': nrm(ks[4], (DEPTH, DEC_BATCH, win, ATT_HEADS, HEAD_DIM), 1.0),
        'cache_v': nrm(ks[5], (DEPTH, DEC_BATCH, win, ATT_HEADS, HEAD_DIM), 1.0),
        'p_prompt': nrm(ks[6], (DEPTH, BATCH, SEQ, D_PLE), 1.0),
        'p_sample': nrm(ks[7], (DEPTH, DEC_BATCH, DEC_SEQ, D_PLE), 1.0),
        'norm_pre_mix': gain(ks[8], D_MODEL),
        'w_in': nrm(ks[9], (DEPTH, D_MODEL, D_IN), D_MODEL ** -0.5),
        'hgrn_lb': nrm(ks[10], (DEPTH, HG_HEADS * HG_DK), 0.5),
        'hgrn_norm': gain(ks[11], HG_HEADS * HG_DV),
        'attn_norm': gain(ks[12], ATT_HEADS * HEAD_DIM),
        'gla_w_gate2': nrm(ks[13], (DEPTH, GLA_GATE_RANK, GLA_HEADS * GLA_DK), GLA_GATE_RANK ** -0.5),
        'gla_b_gate': nrm(ks[14], (DEPTH, GLA_HEADS * GLA_DK), 0.1),
        'gla_norm': gain(ks[15], GLA_HEADS * GLA_DV),
        'w_out': nrm(ks[16], (DEPTH, MIX_WIDTH, D_MODEL), MIX_WIDTH ** -0.5),
        'norm_post_mix': gain(ks[17], D_MODEL),
        'norm_pre_mlp': gain(ks[18], D_MODEL),
        'w_up': nrm(ks[19], (DEPTH, D_MODEL, D_FF), D_MODEL ** -0.5),
        'w_down': nrm(ks[20], (DEPTH, D_FF, D_MODEL), D_FF ** -0.5),
        'norm_post_mlp': gain(ks[21], D_MODEL),
        'w_ple_gate': nrm(ks[22], (DEPTH, D_MODEL, D_MODEL), D_MODEL ** -0.5),
        'w_ple': nrm(ks[23], (DEPTH, D_PLE, D_MODEL), D_PLE ** -0.5),
    }


def reference(x_prompt, x_sample, state_hgrn, state_gla, cache_k, cache_v, p_prompt, p_sample,
              norm_pre_mix, w_in, hgrn_lb, hgrn_norm, attn_norm, gla_w_gate2, gla_b_gate, gla_norm, w_out,
              norm_post_mix, norm_pre_mlp, w_up, w_down, norm_post_mlp, w_ple_gate, w_ple):
    lb_cum = jnp.cumsum(jax.nn.softmax(hgrn_lb.astype(jnp.float32), axis=0), axis=0)
    lower_bounds = lb_cum - lb_cum[0:1]
    bp = x_prompt.shape[0]
    dt = x_prompt.dtype
    zero_hgrn = jnp.zeros((bp, HG_HEADS, HG_DK, HG_DV), dt)
    zero_gla = jnp.zeros((bp, GLA_HEADS, GLA_DK, GLA_DV), dt)
    zero_kv = jnp.zeros((bp, 0, ATT_HEADS, HEAD_DIM), dt)
    hp, hs = x_prompt, x_sample
    hgrn_p, gla_p, kp, vp = [], [], [], []
    hgrn_s, gla_s, ksm, vsm = [], [], [], []
    for i in range(DEPTH):
        w = (norm_pre_mix[i], w_in[i], hgrn_norm[i], attn_norm[i], gla_w_gate2[i], gla_b_gate[i], gla_norm[i],
             w_out[i], norm_post_mix[i], norm_pre_mlp[i], w_up[i], w_down[i], norm_post_mlp[i],
             w_ple_gate[i], w_ple[i])
        hp, sh, sg, kb, vb = trunk_layer(hp, p_prompt[i], zero_hgrn, zero_gla, zero_kv, zero_kv, 0,
                                         lower_bounds[i], *w)
        hgrn_p.append(sh); gla_p.append(sg); kp.append(kb); vp.append(vb)
        hs, sh, sg, kb, vb = trunk_layer(hs, p_sample[i], state_hgrn[i], state_gla[i], cache_k[i], cache_v[i],
                                         PAST_LEN, lower_bounds[i], *w)
        hgrn_s.append(sh); gla_s.append(sg); ksm.append(kb); vsm.append(vb)
    return (hp, hs,
            jnp.stack(hgrn_p), jnp.stack(gla_p), jnp.stack(kp), jnp.stack(vp),
            jnp.stack(hgrn_s), jnp.stack(gla_s), jnp.stack(ksm), jnp.stack(vsm))
```

```python
import functools
import math

import jax
import jax.numpy as jnp
import numpy as np
from jax import lax
from jax.experimental import pallas as pl
from jax.experimental.pallas import tpu as pltpu

F32 = jnp.float32
BF16 = jnp.bfloat16

D_MODEL = 1024
HEAD_DIM = 64
HG_HEADS, HG_DK, HG_DV = 4, 64, 64
ATT_HEADS = 6
GLA_HEADS, GLA_DK, GLA_DV = 6, 32, 64
GLA_GATE_RANK = 16
GLA_TAU = 16.0
D_FF = 4 * D_MODEL
D_PLE = 256
ROPE_THETA = 500000.0
ROT_DIM = HEAD_DIM // 4
DILATED_PATTERNS = ((128, 1), (512, 4), (2048, 16))
MAX_WINDOW = 2048
PAST_LEN = 16384
EPS = 1e-6

HG_W = HG_HEADS * HG_DK
ATT_W = ATT_HEADS * HEAD_DIM
GLA_KW = GLA_HEADS * GLA_DK
GLA_VW = GLA_HEADS * GLA_DV

LANES = 128
CHUNK = 64
QBLK = 128
NEG = -1e30
EXP_CLAMP = 80.0
VMEM_LIMIT = 56 * 1024 * 1024

C_HG = 0
C_AT = C_HG + 4 * HG_W
C_GL = C_AT + 3 * ATT_W
GL_Q, GL_K, GL_V, GL_R, GL_LR = 0, 256, 512, 896, 1280
GL_COLS = 1408
W_COLS = C_GL + GL_COLS


def _rms(x):
    return x * lax.rsqrt(jnp.mean(x * x, axis=-1, keepdims=True) + EPS)


def _sigmoid(x):
    return 1.0 / (1.0 + jnp.exp(-x))


def _dot(a, b):
    return jnp.dot(a, b, preferred_element_type=F32)


def _dot_nt(a, b):
    return lax.dot_general(a, b, (((1,), (1,)), ((), ())), preferred_element_type=F32)


def _dot_tn(a, b):
    return lax.dot_general(a, b, (((0,), (0,)), ((), ())), preferred_element_type=F32)


def _const_spec(shape):
    nd = len(shape)
    return pl.BlockSpec(shape, lambda *_: (0,) * nd, pipeline_mode=pl.Buffered(1))


def _proj_kernel(x_ref, gpre_ref, w_ref, lb_ref, cos_ref, sa_ref, sb_ref, wg2_ref, bg_ref,
                 hq_o, hk_o, hv_o, hlf_o, hgt_o, aq_o, ak_o, av_o,
                 gq_o, gk_o, gv_o, gla_o, ggt_o):
    xn = (_rms(x_ref[...]) * gpre_ref[...]).astype(BF16)

    y = _dot(xn, w_ref[:, C_HG:C_HG + 4 * HG_W])
    lb = lb_ref[...]
    f = lb + (1.0 - lb) * _sigmoid(y[:, HG_W:2 * HG_W])
    hg = y[:, 3 * HG_W:4 * HG_W]
    hq_o[...] = y[:, 0:HG_W].astype(BF16)
    hk_o[...] = (1.0 - f).astype(BF16)
    hv_o[...] = y[:, 2 * HG_W:3 * HG_W].astype(BF16)
    hlf_o[...] = jnp.log(f)
    hgt_o[...] = (hg * _sigmoid(hg)).astype(BF16)

    y = _dot(xn, w_ref[:, C_AT:C_AT + 3 * ATT_W])
    cos_t, sin_a, sin_b = cos_ref[...], sa_ref[...], sb_ref[...]

    def rope(v):
        return v * cos_t + pltpu.roll(v, LANES - ROT_DIM // 2, 1) * sin_a + pltpu.roll(v, ROT_DIM // 2, 1) * sin_b

    for j in range(ATT_W // LANES):
        sl = slice(j * LANES, (j + 1) * LANES)
        aq_o[:, sl] = (rope(y[:, sl]) * (HEAD_DIM ** -0.5)).astype(BF16)
        ak_o[0, :, sl] = rope(y[:, ATT_W + j * LANES:ATT_W + (j + 1) * LANES])
    av_o[0] = y[:, 2 * ATT_W:3 * ATT_W]

    y = _dot(xn, w_ref[:, C_GL:C_GL + GL_COLS])
    gr = y[:, GL_R:GL_R + GLA_VW]
    z = _dot(y[:, GL_LR:GL_LR + LANES].astype(BF16), wg2_ref[...]) + bg_ref[...]
    log_a = (jnp.minimum(z, 0.0) - jnp.log(1.0 + jnp.exp(-jnp.abs(z)))) * (1.0 / GLA_TAU)
    gq_o[...] = (y[:, GL_Q:GL_Q + GLA_KW] * (GLA_DK ** -0.5)).astype(BF16)
    gk_o[...] = y[:, GL_K:GL_K + GLA_KW].astype(BF16)
    gv_o[...] = y[:, GL_V:GL_V + GLA_VW].astype(BF16)
    gla_o[...] = log_a
    ggt_o[...] = (gr * _sigmoid(gr)).astype(BF16)


def _proj_call(x2d, gpre, w_all, lb, rope_tabs, wg2, bg, kv_prev, layer, depth, tm):
    n = x2d.shape[0]
    cos_t, sin_a, sin_b = rope_tabs
    tab_blocks = cos_t.shape[0] // tm
    row = lambda w: pl.BlockSpec((tm, w), lambda i: (i, 0))
    tab = pl.BlockSpec((tm, LANES), lambda i: (i % tab_blocks, 0))
    kv_spec = pl.BlockSpec((1, tm, ATT_W), lambda i: (layer, i, 0))
    in_specs = [row(D_MODEL), _const_spec((1, D_MODEL)), _const_spec((D_MODEL, W_COLS)),
                _const_spec((1, HG_W)), tab, tab, tab, _const_spec((LANES, GLA_KW)), _const_spec((1, GLA_KW))]
    args = [x2d, gpre, w_all, lb, cos_t, sin_a, sin_b, wg2, bg]
    aliases = {}
    if kv_prev is not None:
        in_specs += [pl.BlockSpec(memory_space=pl.ANY)] * 2
        args += list(kv_prev)
        aliases = {len(args) - 2: 6, len(args) - 1: 7}
    sd = jax.ShapeDtypeStruct
    out_shape = [sd((n, HG_W), BF16), sd((n, HG_W), BF16), sd((n, HG_W), BF16), sd((n, HG_W), F32),
                 sd((n, HG_W), BF16), sd((n, ATT_W), BF16), sd((depth, n, ATT_W), F32),
                 sd((depth, n, ATT_W), F32), sd((n, GLA_KW), BF16), sd((n, GLA_KW), BF16),
                 sd((n, GLA_VW), BF16), sd((n, GLA_KW), F32), sd((n, GLA_VW), BF16)]
    out_specs = [row(HG_W)] * 5 + [row(ATT_W), kv_spec, kv_spec, row(GLA_KW), row(GLA_KW),
                                   row(GLA_VW), row(GLA_KW), row(GLA_VW)]

    def body(*refs):
        _proj_kernel(*refs[:9], *refs[len(args):])

    return pl.pallas_call(
        body, grid=(n // tm,), in_specs=in_specs, out_specs=out_specs, out_shape=out_shape,
        input_output_aliases=aliases, name="proj",
        compiler_params=pltpu.CompilerParams(dimension_semantics=("parallel",), vmem_limit_bytes=VMEM_LIMIT),
    )(*args)


def _recur_kernel(heads, dk, dv, has_init, *refs):
    if has_init:
        q_ref, k_ref, v_ref, g_ref, gate_ref, nw_ref, s0_ref, o_ref, s_out_ref, s_scr = refs
    else:
        q_ref, k_ref, v_ref, g_ref, gate_ref, nw_ref, o_ref, s_out_ref, s_scr = refs
    kw, vw = heads * dk, heads * dv
    t_len = q_ref.shape[1]

    def iota(shape, d):
        return lax.broadcasted_iota(jnp.int32, shape, d)

    tri = (iota((CHUNK, CHUNK), 0) >= iota((CHUNK, CHUNK), 1)).astype(BF16)
    causal = (iota((heads * CHUNK, CHUNK), 0) % CHUNK) >= iota((heads * CHUNK, CHUNK), 1)
    stack_mask = (iota((heads * CHUNK, kw), 0) // CHUNK) == (iota((heads * CHUNK, kw), 1) // dk)
    bd_mask = (iota((vw, kw), 0) // dv) == (iota((vw, kw), 1) // dk)
    pool = jnp.where((iota((vw, vw), 0) // dv) == (iota((vw, vw), 1) // dv), 1.0 / dv, 0.0).astype(BF16)
    nw = nw_ref[...]

    if has_init:
        s_scr[...] = s0_ref[0]
    else:
        s_scr[...] = jnp.zeros_like(s_scr)

    def split_dot(a, x, terms):
        acc = None
        for _ in range(terms):
            hi = x.astype(BF16)
            part = _dot(a, hi)
            acc = part if acc is None else acc + part
            x = x - hi.astype(F32)
        return acc

    def chunk(c, carry):
        r0 = pl.multiple_of(c * CHUNK, CHUNK)
        rows = pl.ds(r0, CHUNK)
        q = q_ref[0, rows, :].astype(F32)
        k = k_ref[0, rows, :].astype(F32)
        v = v_ref[0, rows, :]
        g = g_ref[0, rows, :]
        b = split_dot(tri, g, 3)
        b_last = b[CHUNK - 1:CHUNK, :]
        mid = 0.5 * b_last
        q_in = (q * jnp.exp(jnp.minimum(b - mid, EXP_CLAMP))).astype(BF16)
        k_in = (k * jnp.exp(jnp.minimum(mid - b, EXP_CLAMP))).astype(BF16)
        q_st = (q * jnp.exp(b)).astype(BF16)
        k_st = (k * jnp.exp(b_last - b)).astype(BF16)

        q_stack = jnp.where(stack_mask, jnp.concatenate([q_in] * heads, axis=0), 0)
        a = _dot_nt(q_stack, k_in)
        a = jnp.where(causal, a, 0.0).astype(BF16)
        s_t = s_scr[...]
        o = _dot_nt(q_st, s_t.astype(BF16))
        pieces = []
        for j in range(vw // LANES):
            v_pair = v[:, j * LANES:(j + 1) * LANES]
            per_lane = LANES // dv
            acc = None
            for e in range(per_lane):
                h = j * per_lane + e
                oe = _dot(a[h * CHUNK:(h + 1) * CHUNK, :], v_pair)
                sel = (iota((CHUNK, LANES), 1) // dv) == e
                acc = jnp.where(sel, oe, 0.0) if acc is None else jnp.where(sel, oe, acc)
            pieces.append(acc)
        o = o + jnp.concatenate(pieces, axis=1)

        u = _dot_tn(v, k_st)
        s_scr[...] = s_t * jnp.exp(b_last) + jnp.where(bd_mask, u, 0.0)

        ms = split_dot_rhs(o * o, pool)
        o_ref[0, rows, :] = (o * lax.rsqrt(ms + EPS) * nw * gate_ref[0, rows, :].astype(F32)).astype(o_ref.dtype)
        return carry

    def split_dot_rhs(x, p):
        hi = x.astype(BF16)
        lo = (x - hi.astype(F32)).astype(BF16)
        return _dot(hi, p) + _dot(lo, p)

    lax.fori_loop(0, t_len // CHUNK, chunk, 0)
    s_out_ref[0] = s_scr[...]


def _recur_call(q, k, v, g, gate, nw, s0_t, heads, dk, dv, name):
    bsz, t_len, kw = q.shape
    vw = heads * dv
    seq = lambda w: pl.BlockSpec((1, t_len, w), lambda b: (b, 0, 0))
    st = pl.BlockSpec((1, vw, kw), lambda b: (b, 0, 0))
    in_specs = [seq(kw), seq(kw), seq(vw), seq(kw), seq(vw), _const_spec((1, vw))]
    args = [q, k, v, g, gate, nw]
    if s0_t is not None:
        in_specs.append(st)
        args.append(s0_t)
    return pl.pallas_call(
        functools.partial(_recur_kernel, heads, dk, dv, s0_t is not None),
        grid=(bsz,), in_specs=in_specs, out_specs=[seq(vw), st],
        out_shape=[jax.ShapeDtypeStruct((bsz, t_len, vw), BF16), jax.ShapeDtypeStruct((bsz, vw, kw), F32)],
        scratch_shapes=[pltpu.VMEM((vw, kw), F32)], name=name,
        compiler_params=pltpu.CompilerParams(dimension_semantics=("parallel",), vmem_limit_bytes=VMEM_LIMIT),
    )(*args)


def _log_multiplicity(delta):
    delta = np.asarray(delta, np.int64)
    cnt = np.zeros(delta.shape, np.float64)
    for w, d in DILATED_PATTERNS:
        cnt += (delta >= 0) & (delta <= w) & (delta % d == 0)
    return np.where(cnt > 0, np.log(np.maximum(cnt, 1.0)), NEG).astype(np.float32)


def _prompt_attn_kernel(q_ref, k_ref, v_ref, bias_ref, o_ref):
    t_len = q_ref.shape[1]
    nq = t_len // QBLK
    kb = k_ref[0, 0].astype(BF16)
    vb = v_ref[0, 0].astype(BF16)
    lo = lax.broadcasted_iota(jnp.int32, (QBLK, LANES), 1) < HEAD_DIM
    for qb in range(nq):
        klen = (qb + 1) * QBLK
        qblk = q_ref[0, qb * QBLK:(qb + 1) * QBLK, :]
        zero = jnp.zeros_like(qblk)
        q2 = jnp.concatenate([jnp.where(lo, qblk, zero), jnp.where(lo, zero, qblk)], axis=0)
        bias = bias_ref[:, (nq - 1 - qb) * QBLK:t_len]
        s = _dot_nt(q2, kb[0:klen]) + jnp.concatenate([bias, bias], axis=0)
        m = jnp.max(s, axis=1, keepdims=True)
        p = jnp.exp(s - m)
        l = jnp.sum(p, axis=1, keepdims=True)
        o2 = _dot(p.astype(BF16), vb[0:klen]) * (1.0 / l)
        o_ref[0, qb * QBLK:(qb + 1) * QBLK, :] = jnp.where(lo, o2[0:QBLK], o2[QBLK:2 * QBLK]).astype(o_ref.dtype)


def _prompt_attn_call(q, k_all, v_all, layer):
    bsz, t_len, _ = q.shape
    r = np.arange(QBLK)[:, None]
    x = np.arange(t_len)[None, :]
    bias = jnp.asarray(_log_multiplicity(t_len - QBLK + r - x))
    blk = pl.BlockSpec((1, t_len, LANES), lambda b, j: (b, 0, j))
    kv = pl.BlockSpec((1, 1, t_len, LANES), lambda b, j: (layer, b, 0, j))
    return pl.pallas_call(
        _prompt_attn_kernel, grid=(bsz, ATT_W // LANES),
        in_specs=[blk, kv, kv, _const_spec((QBLK, t_len))], out_specs=blk,
        out_shape=jax.ShapeDtypeStruct((bsz, t_len, ATT_W), BF16), name="prompt_attn",
        compiler_params=pltpu.CompilerParams(dimension_semantics=("parallel", "parallel"),
                                             vmem_limit_bytes=VMEM_LIMIT),
    )(q, k_all, v_all, bias)


def _sample_attn_kernel(with_prev, *refs):
    if with_prev:
        q_ref, kn_ref, vn_ref, ck_ref, cv_ref, bias_ref, _, _, o_ref, ok_ref, ov_ref, kb_scr, vb_scr = refs
    else:
        q_ref, kn_ref, vn_ref, ck_ref, cv_ref, bias_ref, o_ref, ok_ref, ov_ref, kb_scr, vb_scr = refs
    win = ck_ref.shape[2]
    t_new = kn_ref.shape[1]
    pad = kb_scr.shape[0] - win
    for c_ref, n_ref, out_ref, scr in ((ck_ref, kn_ref, ok_ref, kb_scr), (cv_ref, vn_ref, ov_ref, vb_scr)):
        new = n_ref[0]
        out_ref[0, 0, 0:win - t_new, :] = c_ref[0, 0, t_new:win, :]
        out_ref[0, 0, win - t_new:win, :] = new
        scr[0:win, :] = c_ref[0, 0].astype(BF16)
        scr[win:win + pad, :] = jnp.concatenate([new, jnp.zeros((pad - t_new, ATT_W), F32)], axis=0).astype(BF16)
    q = q_ref[0].astype(F32)
    rows = ATT_HEADS * t_new
    head_of_row = lax.broadcasted_iota(jnp.int32, (rows, ATT_W), 0) // t_new
    head_of_lane = lax.broadcasted_iota(jnp.int32, (rows, ATT_W), 1) // HEAD_DIM
    own = head_of_row == head_of_lane
    q_stack = jnp.where(own, jnp.concatenate([q] * ATT_HEADS, axis=0), 0.0).astype(BF16)
    s = _dot_nt(q_stack, kb_scr[...]) + bias_ref[...]
    m = jnp.max(s, axis=1, keepdims=True)
    p = jnp.exp(s - m)
    l = jnp.sum(p, axis=1, keepdims=True)
    o_all = jnp.where(own, _dot(p.astype(BF16), vb_scr[...]) * (1.0 / l), 0.0)
    o = o_all[0:t_new]
    for h in range(1, ATT_HEADS):
        o = o + o_all[h * t_new:(h + 1) * t_new]
    o_ref[0] = o.astype(o_ref.dtype)


def _sample_attn_call(q, k_new, v_new, cache_k, cache_v, prev, layer):
    depth, bsz, win, _ = cache_k.shape
    t_new = q.shape[1]
    pad = LANES
    rows = ATT_HEADS * t_new
    tq = np.arange(rows)[:, None] % t_new
    n = np.arange(win + pad)[None, :]
    bias = np.where(n < win + t_new, _log_multiplicity(win + tq - n), NEG).astype(np.float32)
    new = pl.BlockSpec((1, t_new, ATT_W), lambda b: (b, 0, 0))
    cache = pl.BlockSpec((1, 1, win, ATT_W), lambda b: (layer, b, 0, 0))
    in_specs = [new, new, new, cache, cache, _const_spec((rows, win + pad))]
    args = [q, k_new, v_new, cache_k, cache_v, jnp.asarray(bias)]
    aliases = {}
    if prev is not None:
        in_specs += [pl.BlockSpec(memory_space=pl.ANY)] * 2
        args += list(prev)
        aliases = {6: 1, 7: 2}
    sd = jax.ShapeDtypeStruct
    return pl.pallas_call(
        functools.partial(_sample_attn_kernel, prev is not None), grid=(bsz,),
        in_specs=in_specs, out_specs=[new, cache, cache],
        out_shape=[sd((bsz, t_new, ATT_W), BF16), sd(cache_k.shape, F32), sd(cache_v.shape, F32)],
        scratch_shapes=[pltpu.VMEM((win + pad, ATT_W), BF16)] * 2,
        input_output_aliases=aliases, name="sample_attn",
        compiler_params=pltpu.CompilerParams(dimension_semantics=("parallel",), vmem_limit_bytes=VMEM_LIMIT),
    )(*args)


FF_CHUNK = 1024


def _mlp_kernel(h_ref, oh_ref, oa_ref, og_ref, p_ref, an_ref, wo_ref, npm_ref, npre_ref, wup_ref, wdn_ref,
                npost_ref, wpg_ref, wple_ref, out_ref):
    h = h_ref[...]
    oa = (_rms(oa_ref[...].astype(F32)) * an_ref[...]).astype(BF16)
    mix = (_dot(oh_ref[...], wo_ref[0:HG_W, :]) + _dot(oa, wo_ref[HG_W:HG_W + ATT_W, :])
           + _dot(og_ref[...], wo_ref[HG_W + ATT_W:HG_W + ATT_W + GLA_VW, :]))
    h = h + _rms(mix) * npm_ref[...]
    xn = (_rms(h) * npre_ref[...]).astype(BF16)
    acc = None
    for c in range(D_FF // FF_CHUNK):
        u = jnp.maximum(_dot(xn, wup_ref[:, c * FF_CHUNK:(c + 1) * FF_CHUNK]), 0.0)
        part = _dot((u * u).astype(BF16), wdn_ref[c * FF_CHUNK:(c + 1) * FF_CHUNK, :])
        acc = part if acc is None else acc + part
    h = h + _rms(acc) * npost_ref[...]
    gate = _sigmoid(_dot(h.astype(BF16), wpg_ref[...]))
    out_ref[...] = h + gate * _dot(p_ref[...].astype(BF16), wple_ref[...])


def _mlp_call(h2d, oh, oa, og, p2d, an, wo, npm, npre, wup, wdn, npost, wpg, wple, tm):
    n = h2d.shape[0]
    row = lambda w: pl.BlockSpec((tm, w), lambda i: (i, 0))
    vec = _const_spec((1, D_MODEL))
    return pl.pallas_call(
        _mlp_kernel, grid=(n // tm,),
        in_specs=[row(D_MODEL), row(HG_W), row(ATT_W), row(GLA_VW), row(D_PLE), _const_spec((1, ATT_W)),
                  _const_spec((D_MODEL, D_MODEL)), vec, vec, _const_spec((D_MODEL, D_FF)),
                  _const_spec((D_FF, D_MODEL)), vec, _const_spec((D_MODEL, D_MODEL)),
                  _const_spec((D_PLE, D_MODEL))],
        out_specs=row(D_MODEL), out_shape=jax.ShapeDtypeStruct((n, D_MODEL), F32), name="mlp",
        compiler_params=pltpu.CompilerParams(dimension_semantics=("parallel",), vmem_limit_bytes=VMEM_LIMIT),
    )(h2d, oh, oa, og, p2d, an, wo, npm, npre, wup, wdn, npost, wpg, wple)


def _pack_w_in(w_in):
    splits = np.cumsum([0, HG_W, HG_W, HG_W, HG_W, ATT_W, ATT_W, ATT_W, GLA_KW, GLA_KW, GLA_VW,
                        GLA_GATE_RANK, GLA_VW])
    col = lambda i: w_in[:, splits[i]:splits[i + 1]]
    padc = lambda a, w: jnp.pad(a, ((0, 0), (0, w - a.shape[1])))
    parts = [col(0), col(1), col(2), col(3), col(4), col(5), col(6),
             padc(col(7), 256), padc(col(8), 256), col(9), col(11), padc(col(10), LANES)]
    return jnp.concatenate(parts, axis=1).astype(BF16)


def _rope_tables(pos):
    half = ROT_DIM // 2
    inv = jnp.exp(-math.log(ROPE_THETA) * jnp.arange(half, dtype=F32) * (2.0 / ROT_DIM))
    ang = pos[:, None] * inv[None, :]
    cos, sin = jnp.cos(ang), jnp.sin(ang)
    d = np.arange(LANES) % HEAD_DIM
    first, second = d < half, (d >= half) & (d < ROT_DIM)
    idx = np.where(second, d - half, np.where(first, d, 0))
    cos_t = jnp.where(first | second, cos[:, idx], 1.0)
    sin_a = jnp.where(first, -sin[:, idx], 0.0)
    sin_b = jnp.where(second, sin[:, idx], 0.0)
    return cos_t, sin_a, sin_b


def _state_to_bd_t(s):
    bsz, heads, dk, dv = s.shape
    eye = jnp.eye(heads, dtype=s.dtype)
    return jnp.einsum('bhkv,hg->bhvgk', s, eye).reshape(bsz, heads * dv, heads * dk)


def _bd_t_to_state(s_t, heads, dk, dv):
    bsz = s_t.shape[0]
    s5 = s_t.reshape(bsz, heads, dv, heads, dk)
    diag = jnp.stack([s5[:, h, :, h, :] for h in range(heads)], axis=1)
    return jnp.swapaxes(diag, 2, 3)


def kernel(x_prompt, x_sample, state_hgrn, state_gla, cache_k, cache_v, p_prompt, p_sample, norm_pre_mix, w_in, hgrn_lb, hgrn_norm, attn_norm, gla_w_gate2, gla_b_gate, gla_norm, w_out, norm_post_mix, norm_pre_mlp, w_up, w_down, norm_post_mlp, w_ple_gate, w_ple):
    depth = w_in.shape[0]
    bp, tp, _ = x_prompt.shape
    bs, ts, _ = x_sample.shape
    win = cache_k.shape[2]
    past_len = PAST_LEN

    lb_cum = jnp.cumsum(jax.nn.softmax(hgrn_lb.astype(F32), axis=0), axis=0)
    lower_bounds = lb_cum - lb_cum[0:1]

    tm_p = min(512, tp)
    tm_s = bs * ts
    t_pad = CHUNK
    rope_p = _rope_tables(jnp.arange(tp, dtype=F32))
    rope_s = tuple(jnp.tile(t, (bs, 1)) for t in _rope_tables(jnp.arange(ts, dtype=F32) + past_len))
    ck = cache_k.reshape(depth, bs, win, ATT_W)
    cv = cache_v.reshape(depth, bs, win, ATT_W)

    hp = x_prompt.reshape(bp * tp, D_MODEL)
    hs = x_sample.reshape(bs * ts, D_MODEL)
    kv_p = None
    kv_s = None
    hg_p, gl_p, hg_s, gl_s = [], [], [], []
    row = lambda a: a.reshape(1, -1)
    for i in range(depth):
        w_all = _pack_w_in(w_in[i])
        wg2 = jnp.pad(gla_w_gate2[i], ((0, LANES - GLA_GATE_RANK), (0, 0))).astype(BF16)
        wo, wup, wdn = w_out[i].astype(BF16), w_up[i].astype(BF16), w_down[i].astype(BF16)
        wpg, wple = w_ple_gate[i].astype(BF16), w_ple[i].astype(BF16)
        proj_w = (row(norm_pre_mix[i]), w_all, row(lower_bounds[i]))
        mlp_w = (row(attn_norm[i]), wo, row(norm_post_mix[i]), row(norm_pre_mlp[i]), wup, wdn,
                 row(norm_post_mlp[i]), wpg, wple)

        (hq, hk, hv, hlf, hgt, aq, k_all, v_all, gq, gk, gv, gla, ggt) = _proj_call(
            hp, *proj_w, rope_p, wg2, row(gla_b_gate[i]), kv_p, i, depth, tm_p)
        kv_p = (k_all, v_all)
        sh = lambda a: a.reshape(bp, tp, a.shape[-1])
        oh, s_h = _recur_call(sh(hq), sh(hk), sh(hv), sh(hlf), sh(hgt), row(hgrn_norm[i]), None,
                              HG_HEADS, HG_DK, HG_DV, "hgrn")
        og, s_g = _recur_call(sh(gq), sh(gk), sh(gv), sh(gla), sh(ggt), row(gla_norm[i]), None,
                              GLA_HEADS, GLA_DK, GLA_DV, "gla")
        oa = _prompt_attn_call(sh(aq), k_all.reshape(depth, bp, tp, ATT_W), v_all.reshape(depth, bp, tp, ATT_W), i)
        fl = lambda a: a.reshape(bp * tp, a.shape[-1])
        hp = _mlp_call(hp, fl(oh), fl(oa), fl(og), p_prompt[i].reshape(bp * tp, D_PLE), *mlp_w, tm_p)
        hg_p.append(_bd_t_to_state(s_h, HG_HEADS, HG_DK, HG_DV))
        gl_p.append(_bd_t_to_state(s_g, GLA_HEADS, GLA_DK, GLA_DV))

        (hq, hk, hv, hlf, hgt, aq, k_new, v_new, gq, gk, gv, gla, ggt) = _proj_call(
            hs, *proj_w, rope_s, wg2, row(gla_b_gate[i]), None, 0, 1, tm_s)
        pt = lambda a: jnp.pad(a.reshape(bs, ts, a.shape[-1]), ((0, 0), (0, t_pad - ts), (0, 0)))
        oh, s_h = _recur_call(pt(hq), pt(hk), pt(hv), pt(hlf), pt(hgt), row(hgrn_norm[i]),
                              _state_to_bd_t(state_hgrn[i]), HG_HEADS, HG_DK, HG_DV, "hgrn_s")
        og, s_g = _recur_call(pt(gq), pt(gk), pt(gv), pt(gla), pt(ggt), row(gla_norm[i]),
                              _state_to_bd_t(state_gla[i]), GLA_HEADS, GLA_DK, GLA_DV, "gla_s")
        s3 = lambda a: a.reshape(bs, ts, ATT_W)
        oa, ck_new, cv_new = _sample_attn_call(s3(aq), s3(k_new[0]), s3(v_new[0]), ck, cv, kv_s, i)
        kv_s = (ck_new, cv_new)
        ut = lambda a: a[:, :ts].reshape(bs * ts, a.shape[-1])
        hs = _mlp_call(hs, ut(oh), oa.reshape(bs * ts, ATT_W), ut(og), p_sample[i].reshape(bs * ts, D_PLE),
                       *mlp_w, tm_s)
        hg_s.append(_bd_t_to_state(s_h, HG_HEADS, HG_DK, HG_DV))
        gl_s.append(_bd_t_to_state(s_g, GLA_HEADS, GLA_DK, GLA_DV))

    kv5 = lambda a, b, t: a.reshape(depth, b, t, ATT_HEADS, HEAD_DIM)
    return (hp.reshape(bp, tp, D_MODEL), hs.reshape(bs, ts, D_MODEL),
            jnp.stack(hg_p), jnp.stack(gl_p), kv5(kv_p[0], bp, tp), kv5(kv_p[1], bp, tp),
            jnp.stack(hg_s), jnp.stack(gl_s), kv5(kv_s[0], bs, win), kv5(kv_s[1], bs, win))
```

```python
import functools
import math

import jax
import jax.numpy as jnp
import numpy as np
from jax import lax
from jax.experimental import pallas as pl
from jax.experimental.pallas import tpu as pltpu

F32 = jnp.float32
BF16 = jnp.bfloat16

D_MODEL = 1024
HEAD_DIM = 64
HG_HEADS, HG_DK, HG_DV = 4, 64, 64
ATT_HEADS = 6
GLA_HEADS, GLA_DK, GLA_DV = 6, 32, 64
GLA_GATE_RANK = 16
GLA_TAU = 16.0
D_FF = 4 * D_MODEL
D_PLE = 256
ROPE_THETA = 500000.0
ROT_DIM = HEAD_DIM // 4
DILATED_PATTERNS = ((128, 1), (512, 4), (2048, 16))
MAX_WINDOW = 2048
PAST_LEN = 16384
EPS = 1e-6

LANES = 128
HG_W = HG_HEADS * HG_DK
ATT_W = ATT_HEADS * HEAD_DIM
GLA_KW = GLA_HEADS * GLA_DK
GLA_KP = 256
GLA_VW = GLA_HEADS * GLA_DV

CHUNK = 64
RECUR_GROUP = 4
RECUR_TT = 512
QBLK = 128
NEG = -1e30
EXP_CLAMP = 80.0
VMEM_LIMIT = 56 * 1024 * 1024

C_HG = 0
C_AT = C_HG + 4 * HG_W
C_GL = C_AT + 3 * ATT_W
GL_Q, GL_K, GL_V, GL_R, GL_LR = 0, 256, 512, 896, 1280
GL_COLS = 1408
W_COLS = C_GL + GL_COLS


def _rms(x):
    return x * lax.rsqrt(jnp.mean(x * x, axis=-1, keepdims=True) + EPS)


def _sigmoid(x):
    return 1.0 / (1.0 + jnp.exp(-x))


def _dot(a, b):
    return jnp.dot(a, b, preferred_element_type=F32)


def _dot_nt(a, b):
    return lax.dot_general(a, b, (((1,), (1,)), ((), ())), preferred_element_type=F32)


def _dot_tn(a, b):
    return lax.dot_general(a, b, (((0,), (0,)), ((), ())), preferred_element_type=F32)


def _iota(shape, d):
    return lax.broadcasted_iota(jnp.int32, shape, d)


def _const_spec(shape):
    nd = len(shape)
    return pl.BlockSpec(shape, lambda *_: (0,) * nd, pipeline_mode=pl.Buffered(1))


def _params(*semantics):
    return pltpu.CompilerParams(dimension_semantics=semantics, vmem_limit_bytes=VMEM_LIMIT)


def _proj_kernel(kv_transposed, x_ref, gpre_ref, w_ref, lb_ref, cos_ref, sa_ref, sb_ref, wg2_ref, bg_ref,
                 hq_o, hk_o, hv_o, hlf_o, hgt_o, aq_o, ak_o, av_o,
                 gq_o, gk_o, gv_o, gla_o, ggt_o):
    xn = (_rms(x_ref[...]) * gpre_ref[...]).astype(BF16)

    y = _dot(xn, w_ref[:, C_HG:C_HG + 4 * HG_W])
    lb = lb_ref[...]
    f = lb + (1.0 - lb) * _sigmoid(y[:, HG_W:2 * HG_W])
    hg = y[:, 3 * HG_W:4 * HG_W]
    hq_o[...] = y[:, 0:HG_W].astype(BF16)
    hk_o[...] = (1.0 - f).astype(BF16)
    hv_o[...] = y[:, 2 * HG_W:3 * HG_W].astype(BF16)
    hlf_o[...] = jnp.log(f)
    hgt_o[...] = (hg * _sigmoid(hg)).astype(BF16)

    y = _dot(xn, w_ref[:, C_AT:C_AT + 3 * ATT_W])
    cos_t, sin_a, sin_b = cos_ref[...], sa_ref[...], sb_ref[...]

    def rope(v):
        return v * cos_t + pltpu.roll(v, LANES - ROT_DIM // 2, 1) * sin_a + pltpu.roll(v, ROT_DIM // 2, 1) * sin_b

    for j in range(ATT_W // LANES):
        sl = slice(j * LANES, (j + 1) * LANES)
        aq_o[:, sl] = (rope(y[:, sl]) * (HEAD_DIM ** -0.5)).astype(BF16)
        k_rot = rope(y[:, ATT_W + j * LANES:ATT_W + (j + 1) * LANES])
        v_grp = y[:, 2 * ATT_W + j * LANES:2 * ATT_W + (j + 1) * LANES]
        if kv_transposed:
            ak_o[0, 0, sl, :] = k_rot.T
            av_o[0, 0, sl, :] = v_grp.T
        else:
            ak_o[:, sl] = k_rot
            av_o[:, sl] = v_grp

    y = _dot(xn, w_ref[:, C_GL:C_GL + GL_COLS])
    gr = y[:, GL_R:GL_R + GLA_VW]
    z = _dot(y[:, GL_LR:GL_LR + LANES].astype(BF16), wg2_ref[...]) + bg_ref[...]
    log_a = (jnp.minimum(z, 0.0) - jnp.log(1.0 + jnp.exp(-jnp.abs(z)))) * (1.0 / GLA_TAU)
    gq_o[...] = (y[:, GL_Q:GL_Q + GLA_KP] * (GLA_DK ** -0.5)).astype(BF16)
    gk_o[...] = y[:, GL_K:GL_K + GLA_KP].astype(BF16)
    gv_o[...] = y[:, GL_V:GL_V + GLA_VW].astype(BF16)
    gla_o[...] = log_a
    ggt_o[...] = (gr * _sigmoid(gr)).astype(BF16)


def _proj_call(x2d, gpre, w_all, lb, rope_tabs, wg2, bg, kv_prev, layer, depth, tm, seq_len):
    n = x2d.shape[0]
    cos_t, sin_a, sin_b = rope_tabs
    tab_blocks = cos_t.shape[0] // tm
    row = lambda w: pl.BlockSpec((tm, w), lambda i: (i, 0))
    tab = pl.BlockSpec((tm, LANES), lambda i: (i % tab_blocks, 0))
    sd = jax.ShapeDtypeStruct
    if seq_len is not None:
        nt = seq_len // tm
        kv_spec = pl.BlockSpec((1, 1, ATT_W, tm), lambda i: (layer, i // nt, 0, i % nt))
        kv_shape = sd((depth, n // seq_len, ATT_W, seq_len), F32)
    else:
        kv_spec, kv_shape = row(ATT_W), sd((n, ATT_W), F32)
    in_specs = [row(D_MODEL), _const_spec((1, D_MODEL)), _const_spec((D_MODEL, W_COLS)),
                _const_spec((1, HG_W)), tab, tab, tab, _const_spec((LANES, GLA_KP)), _const_spec((1, GLA_KP))]
    args = [x2d, gpre, w_all, lb, cos_t, sin_a, sin_b, wg2, bg]
    aliases = {}
    if kv_prev is not None:
        in_specs += [pl.BlockSpec(memory_space=pl.ANY)] * 2
        args += list(kv_prev)
        aliases = {len(args) - 2: 6, len(args) - 1: 7}
    out_shape = [sd((n, HG_W), BF16), sd((n, HG_W), BF16), sd((n, HG_W), BF16), sd((n, HG_W), F32),
                 sd((n, HG_W), BF16), sd((n, ATT_W), BF16), kv_shape, kv_shape,
                 sd((n, GLA_KP), BF16), sd((n, GLA_KP), BF16),
                 sd((n, GLA_VW), BF16), sd((n, GLA_KP), F32), sd((n, GLA_VW), BF16)]
    out_specs = [row(HG_W)] * 5 + [row(ATT_W), kv_spec, kv_spec, row(GLA_KP), row(GLA_KP),
                                   row(GLA_VW), row(GLA_KP), row(GLA_VW)]
    n_in = len(args)

    def body(*refs):
        _proj_kernel(seq_len is not None, *refs[:9], *refs[n_in:])

    return pl.pallas_call(
        body, grid=(n // tm,), in_specs=in_specs, out_specs=out_specs, out_shape=out_shape,
        input_output_aliases=aliases, name="proj", compiler_params=_params("parallel"),
    )(*args)


def _recur_kernel(heads, dk, dv, has_init, *refs):
    if has_init:
        q_ref, k_ref, v_ref, g_ref, gate_ref, nw_ref, s0_ref, o_ref, s_out_ref, s_scr = refs
    else:
        q_ref, k_ref, v_ref, g_ref, gate_ref, nw_ref, o_ref, s_out_ref, s_scr = refs
    group, t_tile, kwp = q_ref.shape
    kw, vw = heads * dk, heads * dv
    per_k = LANES // dk
    per_v = LANES // dv
    t_idx = pl.program_id(1)

    tri = (_iota((CHUNK, CHUNK), 0) >= _iota((CHUNK, CHUNK), 1)).astype(BF16)
    causal = (_iota((per_k * CHUNK, CHUNK), 0) % CHUNK) >= _iota((per_k * CHUNK, CHUNK), 1)
    own_head = ((_iota((per_k * CHUNK, LANES), 0) // CHUNK) == (_iota((per_k * CHUNK, LANES), 1) // dk)).astype(BF16)
    bd_mask = (_iota((vw, kwp), 0) // dv) == (_iota((vw, kwp), 1) // dk)
    pool = jnp.where((_iota((vw, vw), 0) // dv) == (_iota((vw, vw), 1) // dv), 1.0 / dv, 0.0).astype(BF16)
    v_head = _iota((CHUNK, LANES), 1) // dv
    nw = nw_ref[...]

    @pl.when(t_idx == 0)
    def _():
        for g in range(group):
            if has_init:
                s_nat = s0_ref[g]
                if kwp > kw:
                    s_nat = jnp.concatenate([s_nat, jnp.zeros((kwp - kw, dv), F32)], axis=0)
                tiled = jnp.concatenate([s_nat] * heads, axis=1)
                s_scr[g] = jnp.where(bd_mask, tiled.T, 0.0)
            else:
                s_scr[g] = jnp.zeros((vw, kwp), F32)

    def split_dot(a, x, terms):
        acc = None
        for _ in range(terms):
            hi = x.astype(BF16)
            part = _dot(a, hi)
            acc = part if acc is None else acc + part
            x = x - hi.astype(F32)
        return acc

    k_groups = [(kg, min(per_k, heads - kg * per_k)) for kg in range(kwp // LANES) if heads > kg * per_k]

    def step(c, carry):
        rows = pl.ds(pl.multiple_of(c * CHUNK, CHUNK), CHUNK)
        seqs = range(group)
        b = [split_dot(tri, g_ref[g, rows, :], 3) for g in seqs]
        q_in, k_in, q_st, k_st, dec = [], [], [], [], []
        for g in seqs:
            q = q_ref[g, rows, :].astype(F32)
            k = k_ref[g, rows, :].astype(F32)
            b_last = b[g][CHUNK - 1:CHUNK, :]
            mid = 0.5 * b_last
            q_in.append((q * jnp.exp(jnp.minimum(b[g] - mid, EXP_CLAMP))).astype(BF16))
            k_in.append((k * jnp.exp(jnp.minimum(mid - b[g], EXP_CLAMP))).astype(BF16))
            q_st.append((q * jnp.exp(b[g])).astype(BF16))
            k_st.append((k * jnp.exp(b_last - b[g])).astype(BF16))
            dec.append(jnp.exp(b_last))
        a = []
        for g in seqs:
            per_group = []
            for kg, nh in k_groups:
                sl = slice(kg * LANES, (kg + 1) * LANES)
                stack = jnp.concatenate([q_in[g][:, sl]] * nh, axis=0) * own_head[0:nh * CHUNK]
                per_group.append(_dot_nt(stack, k_in[g][:, sl]))
            a.append(per_group)
        s_old = [s_scr[g] for g in seqs]
        o_inter = [_dot_nt(q_st[g], s_old[g].astype(BF16)) for g in seqs]
        v = [v_ref[g, rows, :] for g in seqs]
        u = [_dot_tn(v[g], k_st[g]) for g in seqs]
        for g in seqs:
            s_scr[g] = s_old[g] * dec[g] + jnp.where(bd_mask, u[g], 0.0)
        o = []
        for g in seqs:
            a_heads = []
            for (kg, nh), ag in zip(k_groups, a[g]):
                am = jnp.where(causal[0:nh * CHUNK], ag, 0.0).astype(BF16)
                a_heads += [am[e * CHUNK:(e + 1) * CHUNK] for e in range(nh)]
            pieces = []
            for j in range(vw // LANES):
                v_grp = v[g][:, j * LANES:(j + 1) * LANES]
                acc = None
                for e in range(per_v):
                    oe = _dot(a_heads[j * per_v + e], v_grp)
                    acc = oe if acc is None else jnp.where(v_head == e, oe, acc)
                pieces.append(acc)
            o.append(o_inter[g] + jnp.concatenate(pieces, axis=1))
        ms = []
        for g in seqs:
            sq = o[g] * o[g]
            sq_hi = sq.astype(BF16)
            ms.append(_dot(sq_hi, pool) + _dot((sq - sq_hi.astype(F32)).astype(BF16), pool))
        for g in seqs:
            gate = gate_ref[g, rows, :].astype(F32)
            o_ref[g, rows, :] = (o[g] * lax.rsqrt(ms[g] + EPS) * nw * gate).astype(o_ref.dtype)
        return carry

    lax.fori_loop(0, t_tile // CHUNK, step, 0)

    @pl.when(t_idx == pl.num_programs(1) - 1)
    def _():
        for g in range(group):
            s_bd = s_scr[g].T
            s_nat = s_bd[:, 0:dv]
            for h in range(1, heads):
                s_nat = s_nat + s_bd[:, h * dv:(h + 1) * dv]
            s_out_ref[g] = s_nat[0:kw]


def _recur_call(q, k, v, g, gate, nw, s0, heads, dk, dv, name):
    bsz, t_len, kwp = q.shape
    kw, vw = heads * dk, heads * dv
    group = math.gcd(RECUR_GROUP, bsz)
    tt = min(RECUR_TT, t_len)
    seq = lambda w: pl.BlockSpec((group, tt, w), lambda b, t: (b, t, 0))
    st = pl.BlockSpec((group, kw, dv), lambda b, t: (b, 0, 0))
    in_specs = [seq(kwp), seq(kwp), seq(vw), seq(kwp), seq(vw), _const_spec((1, vw))]
    args = [q, k, v, g, gate, nw]
    if s0 is not None:
        in_specs.append(st)
        args.append(s0)
    return pl.pallas_call(
        functools.partial(_recur_kernel, heads, dk, dv, s0 is not None),
        grid=(bsz // group, t_len // tt), in_specs=in_specs, out_specs=[seq(vw), st],
        out_shape=[jax.ShapeDtypeStruct((bsz, t_len, vw), BF16), jax.ShapeDtypeStruct((bsz, kw, dv), F32)],
        scratch_shapes=[pltpu.VMEM((group, vw, kwp), F32)], name=name,
        compiler_params=_params("parallel", "arbitrary"),
    )(*args)


def _log_multiplicity(delta):
    delta = np.asarray(delta, np.int64)
    cnt = np.zeros(delta.shape, np.float64)
    for w, d in DILATED_PATTERNS:
        cnt += (delta >= 0) & (delta <= w) & (delta % d == 0)
    return np.where(cnt > 0, np.log(np.maximum(cnt, 1.0)), NEG).astype(np.float32)


def _prompt_attn_kernel(q_ref, k_ref, v_ref, bias_ref, o_ref):
    t_len = q_ref.shape[1]
    nq = t_len // QBLK
    kt = k_ref[0, 0].astype(BF16)
    vt = v_ref[0, 0].astype(BF16)
    lo = _iota((QBLK, LANES), 1) < HEAD_DIM
    for qb in range(nq):
        klen = (qb + 1) * QBLK
        qblk = q_ref[0, qb * QBLK:(qb + 1) * QBLK, :]
        zero = jnp.zeros_like(qblk)
        q2 = jnp.concatenate([jnp.where(lo, qblk, zero), jnp.where(lo, zero, qblk)], axis=0)
        bias = bias_ref[:, (nq - 1 - qb) * QBLK:t_len]
        s = _dot(q2, kt[:, 0:klen]) + jnp.concatenate([bias, bias], axis=0)
        m = jnp.max(s, axis=1, keepdims=True)
        p = jnp.exp(s - m)
        l = jnp.sum(p, axis=1, keepdims=True)
        o2 = _dot_nt(p.astype(BF16), vt[:, 0:klen]) * (1.0 / l)
        o_ref[0, qb * QBLK:(qb + 1) * QBLK, :] = jnp.where(lo, o2[0:QBLK], o2[QBLK:2 * QBLK]).astype(o_ref.dtype)


def _prompt_attn_call(q, k_t, v_t, layer):
    bsz, t_len, _ = q.shape
    r = np.arange(QBLK)[:, None]
    x = np.arange(t_len)[None, :]
    bias = jnp.asarray(_log_multiplicity(t_len - QBLK + r - x))
    blk = pl.BlockSpec((1, t_len, LANES), lambda b, j: (b, 0, j))
    kv = pl.BlockSpec((1, 1, LANES, t_len), lambda b, j: (layer, b, j, 0))
    return pl.pallas_call(
        _prompt_attn_kernel, grid=(bsz, ATT_W // LANES),
        in_specs=[blk, kv, kv, _const_spec((QBLK, t_len))], out_specs=blk,
        out_shape=jax.ShapeDtypeStruct((bsz, t_len, ATT_W), BF16), name="prompt_attn",
        compiler_params=_params("parallel", "parallel"),
    )(q, k_t, v_t, bias)


def _sample_attn_kernel(with_prev, *refs):
    if with_prev:
        q_ref, kn_ref, vn_ref, ck_ref, cv_ref, bias_ref, _, _, o_ref, ok_ref, ov_ref, kb_scr, vb_scr = refs
    else:
        q_ref, kn_ref, vn_ref, ck_ref, cv_ref, bias_ref, o_ref, ok_ref, ov_ref, kb_scr, vb_scr = refs
    win = ck_ref.shape[3]
    t_new = kn_ref.shape[1]
    tail = _iota((LANES, LANES), 1) >= LANES - t_new
    for c_ref, n_ref, out_ref, scr in ((ck_ref, kn_ref, ok_ref, kb_scr), (cv_ref, vn_ref, ov_ref, vb_scr)):
        new_t = jnp.concatenate([n_ref[0], jnp.zeros((LANES - t_new, ATT_W), F32)], axis=0).T
        for rb in range(ATT_W // LANES):
            rows = slice(rb * LANES, (rb + 1) * LANES)
            old = c_ref[0, 0, rows, :]
            shifted = pltpu.roll(old, win - t_new, 1)
            out_ref[0, 0, rows, 0:win - LANES] = shifted[:, 0:win - LANES]
            out_ref[0, 0, rows, win - LANES:win] = jnp.where(
                tail, pltpu.roll(new_t[rows], LANES - t_new, 1), shifted[:, win - LANES:win])
            scr[rows, 0:win] = old.astype(BF16)
            scr[rows, win:win + LANES] = new_t[rows].astype(BF16)
    q = q_ref[0].astype(F32)
    rows = ATT_HEADS * t_new
    own = (_iota((rows, ATT_W), 0) // t_new) == (_iota((rows, ATT_W), 1) // HEAD_DIM)
    q_stack = jnp.where(own, jnp.concatenate([q] * ATT_HEADS, axis=0), 0.0).astype(BF16)
    s = _dot(q_stack, kb_scr[...]) + bias_ref[...]
    m = jnp.max(s, axis=1, keepdims=True)
    p = jnp.exp(s - m)
    l = jnp.sum(p, axis=1, keepdims=True)
    o_all = jnp.where(own, _dot_nt(p.astype(BF16), vb_scr[...]) * (1.0 / l), 0.0)
    o = o_all[0:t_new]
    for h in range(1, ATT_HEADS):
        o = o + o_all[h * t_new:(h + 1) * t_new]
    o_ref[0] = o.astype(o_ref.dtype)


def _sample_attn_call(q, k_new, v_new, cache_kt, cache_vt, prev, layer):
    depth, bsz, _, win = cache_kt.shape
    t_new = q.shape[1]
    rows = ATT_HEADS * t_new
    tq = np.arange(rows)[:, None] % t_new
    n = np.arange(win + LANES)[None, :]
    bias = np.where(n < win + t_new, _log_multiplicity(win + tq - n), NEG).astype(np.float32)
    new = pl.BlockSpec((1, t_new, ATT_W), lambda b: (b, 0, 0))
    cache = pl.BlockSpec((1, 1, ATT_W, win), lambda b: (layer, b, 0, 0))
    in_specs = [new, new, new, cache, cache, _const_spec((rows, win + LANES))]
    args = [q, k_new, v_new, cache_kt, cache_vt, jnp.asarray(bias)]
    aliases = {}
    if prev is not None:
        in_specs += [pl.BlockSpec(memory_space=pl.ANY)] * 2
        args += list(prev)
        aliases = {6: 1, 7: 2}
    sd = jax.ShapeDtypeStruct
    return pl.pallas_call(
        functools.partial(_sample_attn_kernel, prev is not None), grid=(bsz,),
        in_specs=in_specs, out_specs=[new, cache, cache],
        out_shape=[sd((bsz, t_new, ATT_W), BF16), sd(cache_kt.shape, F32), sd(cache_vt.shape, F32)],
        scratch_shapes=[pltpu.VMEM((ATT_W, win + LANES), BF16)] * 2,
        input_output_aliases=aliases, name="sample_attn", compiler_params=_params("parallel"),
    )(*args)


FF_CHUNK = 1024


def _mlp_kernel(h_ref, oh_ref, oa_ref, og_ref, p_ref, an_ref, wo_ref, npm_ref, npre_ref, wup_ref, wdn_ref,
                npost_ref, wpg_ref, wple_ref, out_ref):
    h = h_ref[...]
    oa = (_rms(oa_ref[...].astype(F32)) * an_ref[...]).astype(BF16)
    mix = (_dot(oh_ref[...], wo_ref[0:HG_W, :]) + _dot(oa, wo_ref[HG_W:HG_W + ATT_W, :])
           + _dot(og_ref[...], wo_ref[HG_W + ATT_W:HG_W + ATT_W + GLA_VW, :]))
    h = h + _rms(mix) * npm_ref[...]
    xn = (_rms(h) * npre_ref[...]).astype(BF16)
    acc = None
    for c in range(D_FF // FF_CHUNK):
        u = jnp.maximum(_dot(xn, wup_ref[:, c * FF_CHUNK:(c + 1) * FF_CHUNK]), 0.0)
        part = _dot((u * u).astype(BF16), wdn_ref[c * FF_CHUNK:(c + 1) * FF_CHUNK, :])
        acc = part if acc is None else acc + part
    h = h + _rms(acc) * npost_ref[...]
    gate = _sigmoid(_dot(h.astype(BF16), wpg_ref[...]))
    out_ref[...] = h + gate * _dot(p_ref[...].astype(BF16), wple_ref[...])


def _mlp_call(h2d, oh, oa, og, p3d, layer, an, wo, npm, npre, wup, wdn, npost, wpg, wple, tm):
    n = h2d.shape[0]
    row = lambda w: pl.BlockSpec((tm, w), lambda i: (i, 0))
    vec = _const_spec((1, D_MODEL))
    return pl.pallas_call(
        _mlp_kernel, grid=(n // tm,),
        in_specs=[row(D_MODEL), row(HG_W), row(ATT_W), row(GLA_VW),
                  pl.BlockSpec((None, tm, D_PLE), lambda i: (layer, i, 0)), _const_spec((1, ATT_W)),
                  _const_spec((D_MODEL, D_MODEL)), vec, vec, _const_spec((D_MODEL, D_FF)),
                  _const_spec((D_FF, D_MODEL)), vec, _const_spec((D_MODEL, D_MODEL)),
                  _const_spec((D_PLE, D_MODEL))],
        out_specs=row(D_MODEL), out_shape=jax.ShapeDtypeStruct((n, D_MODEL), F32), name="mlp",
        compiler_params=_params("parallel"),
    )(h2d, oh, oa, og, p3d, an, wo, npm, npre, wup, wdn, npost, wpg, wple)


def _pack_w_in(w_in):
    splits = np.cumsum([0, HG_W, HG_W, HG_W, HG_W, ATT_W, ATT_W, ATT_W, GLA_KW, GLA_KW, GLA_VW,
                        GLA_GATE_RANK, GLA_VW])
    col = lambda i: w_in[:, splits[i]:splits[i + 1]]
    padc = lambda a, w: jnp.pad(a, ((0, 0), (0, w - a.shape[1])))
    parts = [col(0), col(1), col(2), col(3), col(4), col(5), col(6),
             padc(col(7), GLA_KP), padc(col(8), GLA_KP), col(9), col(11), padc(col(10), LANES)]
    return jnp.concatenate(parts, axis=1).astype(BF16)


def _rope_tables(pos):
    half = ROT_DIM // 2
    inv = jnp.exp(-math.log(ROPE_THETA) * jnp.arange(half, dtype=F32) * (2.0 / ROT_DIM))
    ang = pos[:, None] * inv[None, :]
    cos, sin = jnp.cos(ang), jnp.sin(ang)
    d = np.arange(LANES) % HEAD_DIM
    first, second = d < half, (d >= half) & (d < ROT_DIM)
    idx = np.where(second, d - half, np.where(first, d, 0))
    cos_t = jnp.where(first | second, cos[:, idx], 1.0)
    sin_a = jnp.where(first, -sin[:, idx], 0.0)
    sin_b = jnp.where(second, sin[:, idx], 0.0)
    return cos_t, sin_a, sin_b


def kernel(x_prompt, x_sample, state_hgrn, state_gla, cache_k, cache_v, p_prompt, p_sample, norm_pre_mix, w_in, hgrn_lb, hgrn_norm, attn_norm, gla_w_gate2, gla_b_gate, gla_norm, w_out, norm_post_mix, norm_pre_mlp, w_up, w_down, norm_post_mlp, w_ple_gate, w_ple):
    depth = w_in.shape[0]
    bp, tp, _ = x_prompt.shape
    bs, ts, _ = x_sample.shape
    win = cache_k.shape[2]

    lb_cum = jnp.cumsum(jax.nn.softmax(hgrn_lb.astype(F32), axis=0), axis=0)
    lower_bounds = lb_cum - lb_cum[0:1]

    tm_p = min(512, tp)
    tm_s = bs * ts
    rope_p = _rope_tables(jnp.arange(tp, dtype=F32))
    rope_s = tuple(jnp.tile(t, (bs, 1)) for t in _rope_tables(jnp.arange(ts, dtype=F32) + PAST_LEN))
    to_t = lambda a: jnp.transpose(a, (0, 1, 3, 4, 2)).reshape(depth, a.shape[1], ATT_W, a.shape[2])
    from_t = lambda a: jnp.transpose(a.reshape(depth, a.shape[1], ATT_HEADS, HEAD_DIM, a.shape[3]), (0, 1, 4, 2, 3))
    ck_t, cv_t = to_t(cache_k), to_t(cache_v)
    pp3 = p_prompt.reshape(depth, bp * tp, D_PLE)
    ps3 = p_sample.reshape(depth, bs * ts, D_PLE)

    hp = x_prompt.reshape(bp * tp, D_MODEL)
    hs = x_sample.reshape(bs * ts, D_MODEL)
    kv_p = None
    kv_s = None
    hg_p, gl_p, hg_s, gl_s = [], [], [], []
    row = lambda a: a.reshape(1, -1)
    for i in range(depth):
        w_all = _pack_w_in(w_in[i])
        wg2 = jnp.pad(gla_w_gate2[i], ((0, LANES - GLA_GATE_RANK), (0, GLA_KP - GLA_KW))).astype(BF16)
        bg = jnp.pad(gla_b_gate[i], (0, GLA_KP - GLA_KW)).reshape(1, GLA_KP)
        wo, wup, wdn = w_out[i].astype(BF16), w_up[i].astype(BF16), w_down[i].astype(BF16)
        wpg, wple = w_ple_gate[i].astype(BF16), w_ple[i].astype(BF16)
        proj_w = (row(norm_pre_mix[i]), w_all, row(lower_bounds[i]))
        mlp_w = (row(attn_norm[i]), wo, row(norm_post_mix[i]), row(norm_pre_mlp[i]), wup, wdn,
                 row(norm_post_mlp[i]), wpg, wple)

        (hq, hk, hv, hlf, hgt, aq, k_t, v_t, gq, gk, gv, gla, ggt) = _proj_call(
            hp, *proj_w, rope_p, wg2, bg, kv_p, i, depth, tm_p, tp)
        kv_p = (k_t, v_t)
        sh = lambda a: a.reshape(bp, tp, a.shape[-1])
        oh, s_h = _recur_call(sh(hq), sh(hk), sh(hv), sh(hlf), sh(hgt), row(hgrn_norm[i]), None,
                              HG_HEADS, HG_DK, HG_DV, "hgrn")
        og, s_g = _recur_call(sh(gq), sh(gk), sh(gv), sh(gla), sh(ggt), row(gla_norm[i]), None,
                              GLA_HEADS, GLA_DK, GLA_DV, "gla")
        oa = _prompt_attn_call(sh(aq), k_t, v_t, i)
        fl = lambda a: a.reshape(bp * tp, a.shape[-1])
        hp = _mlp_call(hp, fl(oh), fl(oa), fl(og), pp3, i, *mlp_w, tm_p)
        hg_p.append(s_h.reshape(bp, HG_HEADS, HG_DK, HG_DV))
        gl_p.append(s_g.reshape(bp, GLA_HEADS, GLA_DK, GLA_DV))

        (hq, hk, hv, hlf, hgt, aq, k_new, v_new, gq, gk, gv, gla, ggt) = _proj_call(
            hs, *proj_w, rope_s, wg2, bg, None, 0, 1, tm_s, None)
        pt = lambda a: jnp.pad(a.reshape(bs, ts, a.shape[-1]), ((0, 0), (0, CHUNK - ts), (0, 0)))
        oh, s_h = _recur_call(pt(hq), pt(hk), pt(hv), pt(hlf), pt(hgt), row(hgrn_norm[i]),
                              state_hgrn[i].reshape(bs, HG_W, HG_DV), HG_HEADS, HG_DK, HG_DV, "hgrn_s")
        og, s_g = _recur_call(pt(gq), pt(gk), pt(gv), pt(gla), pt(ggt), row(gla_norm[i]),
                              state_gla[i].reshape(bs, GLA_KW, GLA_DV), GLA_HEADS, GLA_DK, GLA_DV, "gla_s")
        s3 = lambda a: a.reshape(bs, ts, ATT_W)
        oa, ck_new, cv_new = _sample_attn_call(s3(aq), s3(k_new), s3(v_new), ck_t, cv_t, kv_s, i)
        kv_s = (ck_new, cv_new)
        ut = lambda a: a[:, :ts].reshape(bs * ts, a.shape[-1])
        hs = _mlp_call(hs, ut(oh), oa.reshape(bs * ts, ATT_W), ut(og), ps3, i, *mlp_w, tm_s)
        hg_s.append(s_h.reshape(bs, HG_HEADS, HG_DK, HG_DV))
        gl_s.append(s_g.reshape(bs, GLA_HEADS, GLA_DK, GLA_DV))

    return (hp.reshape(bp, tp, D_MODEL), hs.reshape(bs, ts, D_MODEL),
            jnp.stack(hg_p), jnp.stack(gl_p), from_t(kv_p[0]), from_t(kv_p[1]),
            jnp.stack(hg_s), jnp.stack(gl_s), from_t(kv_s[0]), from_t(kv_s[1]))
```

```python
import functools
import math

import jax
import jax.numpy as jnp
import numpy as np
from jax import lax
from jax.experimental import pallas as pl
from jax.experimental.pallas import tpu as pltpu

F32 = jnp.float32
BF16 = jnp.bfloat16

D_MODEL = 1024
HEAD_DIM = 64
HG_HEADS, HG_DK, HG_DV = 4, 64, 64
ATT_HEADS = 6
GLA_HEADS, GLA_DK, GLA_DV = 6, 32, 64
GLA_GATE_RANK = 16
GLA_TAU = 16.0
D_FF = 4 * D_MODEL
D_PLE = 256
ROPE_THETA = 500000.0
ROT_DIM = HEAD_DIM // 4
DILATED_PATTERNS = ((128, 1), (512, 4), (2048, 16))
MAX_WINDOW = 2048
PAST_LEN = 16384
EPS = 1e-6

LANES = 128
HG_W = HG_HEADS * HG_DK
ATT_W = ATT_HEADS * HEAD_DIM
GLA_KW = GLA_HEADS * GLA_DK
GLA_KP = 256
GLA_VW = GLA_HEADS * GLA_DV

CHUNK = 64
RECUR_GROUP = 8
RECUR_TT = 512
QBLK = 128
NEG = -1e30
EXP_CLAMP = 80.0
Q_SCALE = HEAD_DIM ** -0.5 * math.log2(math.e)
VMEM_LIMIT = 56 * 1024 * 1024

C_HG = 0
C_AT = C_HG + 4 * HG_W
C_GL = C_AT + 3 * ATT_W
GL_Q, GL_K, GL_V, GL_R, GL_LR = 0, 256, 512, 896, 1280
GL_COLS = 1408
W_COLS = C_GL + GL_COLS


def _rms(x):
    return x * lax.rsqrt(jnp.mean(x * x, axis=-1, keepdims=True) + EPS)


def _sigmoid(x):
    return 1.0 / (1.0 + jnp.exp(-x))


def _dot(a, b):
    return jnp.dot(a, b, preferred_element_type=F32)


def _dot_nt(a, b):
    return lax.dot_general(a, b, (((1,), (1,)), ((), ())), preferred_element_type=F32)


def _dot_tn(a, b):
    return lax.dot_general(a, b, (((0,), (0,)), ((), ())), preferred_element_type=F32)


def _iota(shape, d):
    return lax.broadcasted_iota(jnp.int32, shape, d)


def _const_spec(shape):
    nd = len(shape)
    return pl.BlockSpec(shape, lambda *_: (0,) * nd, pipeline_mode=pl.Buffered(1))


def _params(*semantics):
    return pltpu.CompilerParams(dimension_semantics=semantics, vmem_limit_bytes=VMEM_LIMIT)


def _proj_kernel(streams, x_ref, gpre_ref, w_ref, lb_ref, cos_ref, sa_ref, sb_ref, wg2_ref, bg_ref,
                 hq_o, hk_o, hv_o, hlf_o, hgt_o, gq_o, gk_o, gv_o, gla_o, ggt_o, *att_refs):
    xn = (_rms(x_ref[...]) * gpre_ref[...]).astype(BF16)

    y = _dot(xn, w_ref[:, C_HG:C_HG + 4 * HG_W])
    lb = lb_ref[...]
    f = lb + (1.0 - lb) * _sigmoid(y[:, HG_W:2 * HG_W])
    hg = y[:, 3 * HG_W:4 * HG_W]
    hq_o[...] = y[:, 0:HG_W].astype(BF16)
    hk_o[...] = (1.0 - f).astype(BF16)
    hv_o[...] = y[:, 2 * HG_W:3 * HG_W].astype(BF16)
    hlf_o[...] = jnp.log(f)
    hgt_o[...] = (hg * _sigmoid(hg)).astype(BF16)

    y = _dot(xn, w_ref[:, C_AT:C_AT + 3 * ATT_W])
    cos_t, sin_a, sin_b = cos_ref[...], sa_ref[...], sb_ref[...]

    def rope(v):
        return v * cos_t + pltpu.roll(v, LANES - ROT_DIM // 2, 1) * sin_a + pltpu.roll(v, ROT_DIM // 2, 1) * sin_b

    if streams:
        *stream_os, ak_o, av_o, att_scr = att_refs
    else:
        aq_o, ak_o, av_o = att_refs
    for j in range(ATT_W // LANES):
        sl = slice(j * LANES, (j + 1) * LANES)
        q_rot = rope(y[:, sl]) * Q_SCALE
        k_rot = rope(y[:, ATT_W + j * LANES:ATT_W + (j + 1) * LANES])
        v_grp = y[:, 2 * ATT_W + j * LANES:2 * ATT_W + (j + 1) * LANES]
        if streams:
            ak_o[0, 0, sl, :] = k_rot.T
            av_o[0, 0, sl, :] = v_grp.T
            n_grp = ATT_W // LANES
            att_scr[j], att_scr[n_grp + j], att_scr[2 * n_grp + j] = q_rot, k_rot, v_grp
        else:
            aq_o[:, sl] = q_rot
            ak_o[:, sl] = k_rot
            av_o[:, sl] = v_grp
    if streams:
        for s_o in stream_os:
            d, n = s_o.shape[1], s_o.shape[2]
            for r in range(d):
                rows = pl.ds(r, n, stride=d) if d > 1 else slice(None)
                for grp in range(att_scr.shape[0]):
                    s_o[0, r, :, grp * LANES:(grp + 1) * LANES] = att_scr[grp, rows, :].astype(BF16)

    y = _dot(xn, w_ref[:, C_GL:C_GL + GL_COLS])
    gr = y[:, GL_R:GL_R + GLA_VW]
    z = _dot(y[:, GL_LR:GL_LR + LANES].astype(BF16), wg2_ref[...]) + bg_ref[...]
    log_a = (jnp.minimum(z, 0.0) - jnp.log(1.0 + jnp.exp(-jnp.abs(z)))) * (1.0 / GLA_TAU)
    gq_o[...] = (y[:, GL_Q:GL_Q + GLA_KP] * (GLA_DK ** -0.5)).astype(BF16)
    gk_o[...] = y[:, GL_K:GL_K + GLA_KP].astype(BF16)
    gv_o[...] = y[:, GL_V:GL_V + GLA_VW].astype(BF16)
    gla_o[...] = log_a
    ggt_o[...] = (gr * _sigmoid(gr)).astype(BF16)


def _proj_call(x2d, gpre, w_all, lb, rope_tabs, wg2, bg, kv_prev, layer, depth, tm, seq_len):
    n = x2d.shape[0]
    cos_t, sin_a, sin_b = rope_tabs
    tab_blocks = cos_t.shape[0] // tm
    row = lambda w: pl.BlockSpec((tm, w), lambda i: (i, 0))
    tab = pl.BlockSpec((tm, LANES), lambda i: (i % tab_blocks, 0))
    sd = jax.ShapeDtypeStruct
    in_specs = [row(D_MODEL), _const_spec((1, D_MODEL)), _const_spec((D_MODEL, W_COLS)),
                _const_spec((1, HG_W)), tab, tab, tab, _const_spec((LANES, GLA_KP)), _const_spec((1, GLA_KP))]
    args = [x2d, gpre, w_all, lb, cos_t, sin_a, sin_b, wg2, bg]
    out_shape = [sd((n, HG_W), BF16), sd((n, HG_W), BF16), sd((n, HG_W), BF16), sd((n, HG_W), F32),
                 sd((n, HG_W), BF16), sd((n, GLA_KP), BF16), sd((n, GLA_KP), BF16),
                 sd((n, GLA_VW), BF16), sd((n, GLA_KP), F32), sd((n, GLA_VW), BF16)]
    out_specs = [row(HG_W)] * 5 + [row(GLA_KP), row(GLA_KP), row(GLA_VW), row(GLA_KP), row(GLA_VW)]
    aliases, scratch = {}, []
    if seq_len is not None:
        nt, bsz = seq_len // tm, n // seq_len
        for _, d in DILATED_PATTERNS:
            out_shape.append(sd((bsz, d, seq_len // d, 3 * ATT_W), BF16))
            out_specs.append(pl.BlockSpec((1, d, tm // d, 3 * ATT_W), lambda i: (i // nt, 0, i % nt, 0)))
        out_shape += [sd((depth, bsz, ATT_W, seq_len), F32)] * 2
        out_specs += [pl.BlockSpec((1, 1, ATT_W, tm), lambda i: (layer, i // nt, 0, i % nt))] * 2
        scratch = [pltpu.VMEM((3 * ATT_W // LANES, tm, LANES), F32)]
        if kv_prev is not None:
            in_specs += [pl.BlockSpec(memory_space=pl.ANY)] * 2
            args += list(kv_prev)
            aliases = {len(args) - 2: len(out_shape) - 2, len(args) - 1: len(out_shape) - 1}
    else:
        out_shape += [sd((n, ATT_W), F32)] * 3
        out_specs += [row(ATT_W)] * 3
    n_in = len(args)

    def body(*refs):
        _proj_kernel(seq_len is not None, *refs[:9], *refs[n_in:])

    return pl.pallas_call(
        body, grid=(n // tm,), in_specs=in_specs, out_specs=out_specs, out_shape=out_shape,
        scratch_shapes=scratch, input_output_aliases=aliases, name="proj", compiler_params=_params("parallel"),
    )(*args)


def _recur_kernel(heads, dk, dv, has_init, *refs):
    if has_init:
        q_ref, k_ref, v_ref, g_ref, gate_ref, nw_ref, s0_ref, o_ref, s_out_ref, s_scr = refs
    else:
        q_ref, k_ref, v_ref, g_ref, gate_ref, nw_ref, o_ref, s_out_ref, s_scr = refs
    group, t_tile, kwp = q_ref.shape
    kw, vw = heads * dk, heads * dv
    per_k = LANES // dk
    per_v = LANES // dv
    t_idx = pl.program_id(1)

    tri = (_iota((CHUNK, CHUNK), 0) >= _iota((CHUNK, CHUNK), 1)).astype(BF16)
    causal = (_iota((per_k * CHUNK, CHUNK), 0) % CHUNK) >= _iota((per_k * CHUNK, CHUNK), 1)
    own_head = ((_iota((per_k * CHUNK, LANES), 0) // CHUNK) == (_iota((per_k * CHUNK, LANES), 1) // dk)).astype(BF16)
    bd_mask = (_iota((vw, kwp), 0) // dv) == (_iota((vw, kwp), 1) // dk)
    pool = jnp.where((_iota((vw, vw), 0) // dv) == (_iota((vw, vw), 1) // dv), 1.0 / dv, 0.0).astype(BF16)
    v_head = _iota((CHUNK, LANES), 1) // dv
    nw = nw_ref[...]

    @pl.when(t_idx == 0)
    def _():
        for g in range(group):
            if has_init:
                s_nat = s0_ref[g]
                if kwp > kw:
                    s_nat = jnp.concatenate([s_nat, jnp.zeros((kwp - kw, dv), F32)], axis=0)
                tiled = jnp.concatenate([s_nat] * heads, axis=1)
                s_scr[g] = jnp.where(bd_mask, tiled.T, 0.0)
            else:
                s_scr[g] = jnp.zeros((vw, kwp), F32)

    def split_dot(a, x, terms):
        acc = None
        for _ in range(terms):
            hi = x.astype(BF16)
            part = _dot(a, hi)
            acc = part if acc is None else acc + part
            x = x - hi.astype(F32)
        return acc

    k_groups = [(kg, min(per_k, heads - kg * per_k)) for kg in range(kwp // LANES) if heads > kg * per_k]

    def step(c, carry):
        rows = pl.ds(pl.multiple_of(c * CHUNK, CHUNK), CHUNK)
        seqs = range(group)
        b = [split_dot(tri, g_ref[g, rows, :], 2) for g in seqs]
        q_in, k_in, q_st, k_st, dec = [], [], [], [], []
        for g in seqs:
            q = q_ref[g, rows, :].astype(F32)
            k = k_ref[g, rows, :].astype(F32)
            b_last = b[g][CHUNK - 1:CHUNK, :]
            mid = 0.5 * b_last
            q_in.append((q * jnp.exp(jnp.minimum(b[g] - mid, EXP_CLAMP))).astype(BF16))
            k_in.append((k * jnp.exp(jnp.minimum(mid - b[g], EXP_CLAMP))).astype(BF16))
            q_st.append((q * jnp.exp(b[g])).astype(BF16))
            k_st.append((k * jnp.exp(b_last - b[g])).astype(BF16))
            dec.append(jnp.exp(b_last))
        a = []
        for g in seqs:
            per_group = []
            for kg, nh in k_groups:
                sl = slice(kg * LANES, (kg + 1) * LANES)
                stack = jnp.concatenate([q_in[g][:, sl]] * nh, axis=0) * own_head[0:nh * CHUNK]
                per_group.append(_dot_nt(stack, k_in[g][:, sl]))
            a.append(per_group)
        s_old = [s_scr[g] for g in seqs]
        o_inter = [_dot_nt(q_st[g], s_old[g].astype(BF16)) for g in seqs]
        v = [v_ref[g, rows, :] for g in seqs]
        u = [_dot_tn(v[g], k_st[g]) for g in seqs]
        for g in seqs:
            s_scr[g] = s_old[g] * dec[g] + jnp.where(bd_mask, u[g], 0.0)
        o = []
        for g in seqs:
            a_heads = []
            for (kg, nh), ag in zip(k_groups, a[g]):
                am = jnp.where(causal[0:nh * CHUNK], ag, 0.0).astype(BF16)
                a_heads += [am[e * CHUNK:(e + 1) * CHUNK] for e in range(nh)]
            pieces = []
            for j in range(vw // LANES):
                v_grp = v[g][:, j * LANES:(j + 1) * LANES]
                acc = None
                for e in range(per_v):
                    oe = _dot(a_heads[j * per_v + e], v_grp)
                    acc = oe if acc is None else jnp.where(v_head == e, oe, acc)
                pieces.append(acc)
            o.append(o_inter[g] + jnp.concatenate(pieces, axis=1))
        ms = [_dot((o[g] * o[g]).astype(BF16), pool) for g in seqs]
        for g in seqs:
            gate = gate_ref[g, rows, :].astype(F32)
            o_ref[g, rows, :] = (o[g] * lax.rsqrt(ms[g] + EPS) * nw * gate).astype(o_ref.dtype)
        return carry

    lax.fori_loop(0, t_tile // CHUNK, step, 0)

    @pl.when(t_idx == pl.num_programs(1) - 1)
    def _():
        for g in range(group):
            s_bd = s_scr[g].T
            s_nat = s_bd[:, 0:dv]
            for h in range(1, heads):
                s_nat = s_nat + s_bd[:, h * dv:(h + 1) * dv]
            s_out_ref[g] = s_nat[0:kw]


def _recur_call(q, k, v, g, gate, nw, s0, heads, dk, dv, name):
    bsz, t_len, kwp = q.shape
    kw, vw = heads * dk, heads * dv
    group = math.gcd(RECUR_GROUP, bsz)
    tt = min(RECUR_TT, t_len)
    seq = lambda w: pl.BlockSpec((group, tt, w), lambda b, t: (b, t, 0))
    st = pl.BlockSpec((group, kw, dv), lambda b, t: (b, 0, 0))
    in_specs = [seq(kwp), seq(kwp), seq(vw), seq(kwp), seq(vw), _const_spec((1, vw))]
    args = [q, k, v, g, gate, nw]
    if s0 is not None:
        in_specs.append(st)
        args.append(s0)
    return pl.pallas_call(
        functools.partial(_recur_kernel, heads, dk, dv, s0 is not None),
        grid=(bsz // group, t_len // tt), in_specs=in_specs, out_specs=[seq(vw), st],
        out_shape=[jax.ShapeDtypeStruct((bsz, t_len, vw), BF16), jax.ShapeDtypeStruct((bsz, kw, dv), F32)],
        scratch_shapes=[pltpu.VMEM((group, vw, kwp), F32)], name=name,
        compiler_params=_params("parallel", "arbitrary"),
    )(*args)


def _log_multiplicity(delta):
    delta = np.asarray(delta, np.int64)
    cnt = np.zeros(delta.shape, np.float64)
    for w, d in DILATED_PATTERNS:
        cnt += (delta >= 0) & (delta <= w) & (delta % d == 0)
    return np.where(cnt > 0, np.log2(np.maximum(cnt, 1.0)), NEG).astype(np.float32)


ATTN_UNITS_PER_GROUP = 8


def _prompt_attn_kernel(*refs):
    n_pat = len(DILATED_PATTERNS)
    qs, ks, vs = refs[0:3 * n_pat:3], refs[1:3 * n_pat:3], refs[2:3 * n_pat:3]
    mask_ref, o_ref = refs[3 * n_pat:3 * n_pat + 2]
    od, zd = refs[3 * n_pat + 2::2], refs[3 * n_pat + 3::2]
    t_len = o_ref.shape[1]
    dils = [d for _, d in DILATED_PATTERNS]
    lo = _iota((QBLK, LANES), 1) < HEAD_DIM

    units = []
    for di, d in enumerate(dils):
        for r in range(d):
            for pb in range(t_len // d // QBLK):
                q0 = pb * QBLK
                k0, klen = (q0, QBLK) if pb == 0 else (q0 - QBLK, 2 * QBLK)
                units.append((di, r, q0, k0, klen))

    eye = (_iota((QBLK, QBLK), 0) == _iota((QBLK, QBLK), 1)).astype(BF16)
    mask_t = mask_ref[...]

    def run(group):
        chains = [(u, e) for u in group for e in (0, 1)]
        s = []
        for (di, r, q0, k0, klen), e in chains:
            q = qs[di][0, r, q0:q0 + QBLK, :]
            q = jnp.where(lo, q, jnp.zeros_like(q)) if e == 0 else jnp.where(lo, jnp.zeros_like(q), q)
            k_aug = jnp.concatenate([ks[di][0, r, k0:k0 + klen, :], mask_t[2 * QBLK - klen:2 * QBLK]], axis=1)
            s.append(_dot_nt(jnp.concatenate([q, eye], axis=1), k_aug))
        m = [jnp.max(x, axis=1, keepdims=True) for x in s]
        p = [jnp.exp2(x - mx) for x, mx in zip(s, m)]
        l = [jnp.sum(x, axis=1, keepdims=True) for x in p]
        num = [_dot(x.astype(BF16), vs[di][0, r, k0:k0 + klen, :])
               for x, ((di, r, _, k0, klen), _) in zip(p, chains)]
        out = [x * (1.0 / lx) for x, lx in zip(num, l)]
        lse = [mx + jnp.log2(lx) for mx, lx in zip(m, l)]
        for i, (di, r, q0, _, _) in enumerate(group):
            od[di][r, q0:q0 + QBLK, :] = jnp.where(lo, out[2 * i], out[2 * i + 1])
            zd[di][r, q0:q0 + QBLK, :] = jnp.where(lo, lse[2 * i], lse[2 * i + 1])

    for i in range(0, len(units), ATTN_UNITS_PER_GROUP):
        run(units[i:i + ATTN_UNITS_PER_GROUP])

    d_max = dils[-1]
    n_max = t_len // d_max
    for r in range(d_max):
        def rows_of(di):
            d = dils[di]
            return (r % d, pl.ds(r // d, n_max, stride=d_max // d) if d < d_max else slice(None))
        z = [zd[di][rows_of(di)] for di in range(n_pat)]
        top = functools.reduce(jnp.maximum, z)
        w = [jnp.exp2(x - top) for x in z]
        acc = sum(wx * od[di][rows_of(di)] for di, wx in enumerate(w))
        o_ref[0, pl.ds(r, n_max, stride=d_max), :] = acc * (1.0 / sum(w))


def _prompt_attn_call(streams, t_len):
    bsz = streams[0].shape[0]
    assert all(w // d == QBLK for w, d in DILATED_PATTERNS) and t_len % (QBLK * max(d for _, d in DILATED_PATTERNS)) == 0
    a = np.arange(QBLK)[:, None]
    c = np.arange(2 * QBLK)[None, :]
    mask_t = jnp.asarray(np.where((c >= a) & (c <= a + QBLK), 0.0, NEG).astype(np.float32).T, dtype=BF16)
    n_grp = ATT_W // LANES
    in_specs, args = [], []
    for s_d in streams:
        d, n = s_d.shape[1], s_d.shape[2]
        for part in range(3):
            in_specs.append(pl.BlockSpec((1, d, n, LANES), lambda b, j, part=part: (b, 0, 0, part * n_grp + j)))
            args.append(s_d)
    scratch = []
    for _, d in DILATED_PATTERNS:
        scratch += [pltpu.VMEM((d, t_len // d, LANES), F32)] * 2
    return pl.pallas_call(
        _prompt_attn_kernel, grid=(bsz, n_grp),
        in_specs=in_specs + [_const_spec((2 * QBLK, QBLK))],
        out_specs=pl.BlockSpec((1, t_len, LANES), lambda b, j: (b, 0, j)),
        out_shape=jax.ShapeDtypeStruct((bsz, t_len, ATT_W), F32), name="prompt_attn",
        scratch_shapes=scratch,
        compiler_params=_params("parallel", "parallel"),
    )(*args, mask_t)


def _sample_attn_kernel(with_prev, *refs):
    if with_prev:
        q_ref, kn_ref, vn_ref, ck_ref, cv_ref, bias_ref, _, _, o_ref, ok_ref, ov_ref, kb_scr, vb_scr = refs
    else:
        q_ref, kn_ref, vn_ref, ck_ref, cv_ref, bias_ref, o_ref, ok_ref, ov_ref, kb_scr, vb_scr = refs
    win = ck_ref.shape[3]
    t_new = kn_ref.shape[1]
    tail = _iota((LANES, LANES), 1) >= LANES - t_new
    for c_ref, n_ref, out_ref, scr in ((ck_ref, kn_ref, ok_ref, kb_scr), (cv_ref, vn_ref, ov_ref, vb_scr)):
        new_t = jnp.concatenate([n_ref[0], jnp.zeros((LANES - t_new, ATT_W), F32)], axis=0).T
        for rb in range(ATT_W // LANES):
            rows = slice(rb * LANES, (rb + 1) * LANES)
            old = c_ref[0, 0, rows, :]
            shifted = pltpu.roll(old, win - t_new, 1)
            out_ref[0, 0, rows, 0:win - LANES] = shifted[:, 0:win - LANES]
            out_ref[0, 0, rows, win - LANES:win] = jnp.where(
                tail, pltpu.roll(new_t[rows], LANES - t_new, 1), shifted[:, win - LANES:win])
            scr[rows, 0:win] = old.astype(BF16)
            scr[rows, win:win + LANES] = new_t[rows].astype(BF16)
    q = q_ref[0].astype(F32)
    rows = ATT_HEADS * t_new
    own = (_iota((rows, ATT_W), 0) // t_new) == (_iota((rows, ATT_W), 1) // HEAD_DIM)
    q_stack = jnp.where(own, jnp.concatenate([q] * ATT_HEADS, axis=0), 0.0).astype(BF16)
    s = _dot(q_stack, kb_scr[...]) + bias_ref[...]
    m = jnp.max(s, axis=1, keepdims=True)
    p = jnp.exp2(s - m)
    l = jnp.sum(p, axis=1, keepdims=True)
    o_all = jnp.where(own, _dot_nt(p.astype(BF16), vb_scr[...]) * (1.0 / l), 0.0)
    o = o_all[0:t_new]
    for h in range(1, ATT_HEADS):
        o = o + o_all[h * t_new:(h + 1) * t_new]
    o_ref[0] = o.astype(o_ref.dtype)


def _sample_attn_call(q, k_new, v_new, cache_kt, cache_vt, prev, layer):
    depth, bsz, _, win = cache_kt.shape
    t_new = q.shape[1]
    rows = ATT_HEADS * t_new
    tq = np.arange(rows)[:, None] % t_new
    n = np.arange(win + LANES)[None, :]
    bias = np.where(n < win + t_new, _log_multiplicity(win + tq - n), NEG).astype(np.float32)
    new = pl.BlockSpec((1, t_new, ATT_W), lambda b: (b, 0, 0))
    cache = pl.BlockSpec((1, 1, ATT_W, win), lambda b: (layer, b, 0, 0))
    in_specs = [new, new, new, cache, cache, _const_spec((rows, win + LANES))]
    args = [q, k_new, v_new, cache_kt, cache_vt, jnp.asarray(bias)]
    aliases = {}
    if prev is not None:
        in_specs += [pl.BlockSpec(memory_space=pl.ANY)] * 2
        args += list(prev)
        aliases = {6: 1, 7: 2}
    sd = jax.ShapeDtypeStruct
    return pl.pallas_call(
        functools.partial(_sample_attn_kernel, prev is not None), grid=(bsz,),
        in_specs=in_specs, out_specs=[new, cache, cache],
        out_shape=[sd((bsz, t_new, ATT_W), BF16), sd(cache_kt.shape, F32), sd(cache_vt.shape, F32)],
        scratch_shapes=[pltpu.VMEM((ATT_W, win + LANES), BF16)] * 2,
        input_output_aliases=aliases, name="sample_attn", compiler_params=_params("parallel"),
    )(*args)


FF_CHUNK = 1024


def _mlp_kernel(h_ref, oh_ref, oa_ref, og_ref, p_ref, an_ref, wo_ref, npm_ref, npre_ref, wup_ref, wdn_ref,
                npost_ref, wpg_ref, wple_ref, out_ref):
    h = h_ref[...]
    oa = (_rms(oa_ref[...].astype(F32)) * an_ref[...]).astype(BF16)
    mix = (_dot(oh_ref[...], wo_ref[0:HG_W, :]) + _dot(oa, wo_ref[HG_W:HG_W + ATT_W, :])
           + _dot(og_ref[...], wo_ref[HG_W + ATT_W:HG_W + ATT_W + GLA_VW, :]))
    h = h + _rms(mix) * npm_ref[...]
    xn = (_rms(h) * npre_ref[...]).astype(BF16)
    acc = None
    for c in range(D_FF // FF_CHUNK):
        u = jnp.maximum(_dot(xn, wup_ref[:, c * FF_CHUNK:(c + 1) * FF_CHUNK]), 0.0)
        part = _dot((u * u).astype(BF16), wdn_ref[c * FF_CHUNK:(c + 1) * FF_CHUNK, :])
        acc = part if acc is None else acc + part
    h = h + _rms(acc) * npost_ref[...]
    gate = _sigmoid(_dot(h.astype(BF16), wpg_ref[...]))
    out_ref[...] = h + gate * _dot(p_ref[...].astype(BF16), wple_ref[...])


def _mlp_call(h2d, oh, oa, og, p3d, layer, an, wo, npm, npre, wup, wdn, npost, wpg, wple, tm):
    n = h2d.shape[0]
    row = lambda w: pl.BlockSpec((tm, w), lambda i: (i, 0))
    vec = _const_spec((1, D_MODEL))
    return pl.pallas_call(
        _mlp_kernel, grid=(n // tm,),
        in_specs=[row(D_MODEL), row(HG_W), row(ATT_W), row(GLA_VW),
                  pl.BlockSpec((None, tm, D_PLE), lambda i: (layer, i, 0)), _const_spec((1, ATT_W)),
                  _const_spec((D_MODEL, D_MODEL)), vec, vec, _const_spec((D_MODEL, D_FF)),
                  _const_spec((D_FF, D_MODEL)), vec, _const_spec((D_MODEL, D_MODEL)),
                  _const_spec((D_PLE, D_MODEL))],
        out_specs=row(D_MODEL), out_shape=jax.ShapeDtypeStruct((n, D_MODEL), F32), name="mlp",
        compiler_params=_params("parallel"),
    )(h2d, oh, oa, og, p3d, an, wo, npm, npre, wup, wdn, npost, wpg, wple)


def _pack_w_in(w_in):
    splits = np.cumsum([0, HG_W, HG_W, HG_W, HG_W, ATT_W, ATT_W, ATT_W, GLA_KW, GLA_KW, GLA_VW,
                        GLA_GATE_RANK, GLA_VW])
    col = lambda i: w_in[:, splits[i]:splits[i + 1]]
    padc = lambda a, w: jnp.pad(a, ((0, 0), (0, w - a.shape[1])))
    parts = [col(0), col(1), col(2), col(3), col(4), col(5), col(6),
             padc(col(7), GLA_KP), padc(col(8), GLA_KP), col(9), col(11), padc(col(10), LANES)]
    return jnp.concatenate(parts, axis=1).astype(BF16)


def _rope_tables(pos):
    half = ROT_DIM // 2
    inv = jnp.exp(-math.log(ROPE_THETA) * jnp.arange(half, dtype=F32) * (2.0 / ROT_DIM))
    ang = pos[:, None] * inv[None, :]
    cos, sin = jnp.cos(ang), jnp.sin(ang)
    d = np.arange(LANES) % HEAD_DIM
    first, second = d < half, (d >= half) & (d < ROT_DIM)
    idx = np.where(second, d - half, np.where(first, d, 0))
    cos_t = jnp.where(first | second, cos[:, idx], 1.0)
    sin_a = jnp.where(first, -sin[:, idx], 0.0)
    sin_b = jnp.where(second, sin[:, idx], 0.0)
    return cos_t, sin_a, sin_b


def kernel(x_prompt, x_sample, state_hgrn, state_gla, cache_k, cache_v, p_prompt, p_sample, norm_pre_mix, w_in, hgrn_lb, hgrn_norm, attn_norm, gla_w_gate2, gla_b_gate, gla_norm, w_out, norm_post_mix, norm_pre_mlp, w_up, w_down, norm_post_mlp, w_ple_gate, w_ple):
    depth = w_in.shape[0]
    bp, tp, _ = x_prompt.shape
    bs, ts, _ = x_sample.shape
    win = cache_k.shape[2]

    lb_cum = jnp.cumsum(jax.nn.softmax(hgrn_lb.astype(F32), axis=0), axis=0)
    lower_bounds = lb_cum - lb_cum[0:1]

    tm_p = min(512, tp)
    tm_s = bs * ts
    rope_p = _rope_tables(jnp.arange(tp, dtype=F32))
    rope_s = tuple(jnp.tile(t, (bs, 1)) for t in _rope_tables(jnp.arange(ts, dtype=F32) + PAST_LEN))
    to_t = lambda a: jnp.transpose(a, (0, 1, 3, 4, 2)).reshape(depth, a.shape[1], ATT_W, a.shape[2])
    from_t = lambda a: jnp.transpose(a.reshape(depth, a.shape[1], ATT_HEADS, HEAD_DIM, a.shape[3]), (0, 1, 4, 2, 3))
    ck_t, cv_t = to_t(cache_k), to_t(cache_v)
    pp3 = p_prompt.reshape(depth, bp * tp, D_PLE)
    ps3 = p_sample.reshape(depth, bs * ts, D_PLE)

    hp = x_prompt.reshape(bp * tp, D_MODEL)
    hs = x_sample.reshape(bs * ts, D_MODEL)
    kv_p = None
    kv_s = None
    hg_p, gl_p, hg_s, gl_s = [], [], [], []
    row = lambda a: a.reshape(1, -1)
    for i in range(depth):
        w_all = _pack_w_in(w_in[i])
        wg2 = jnp.pad(gla_w_gate2[i], ((0, LANES - GLA_GATE_RANK), (0, GLA_KP - GLA_KW))).astype(BF16)
        bg = jnp.pad(gla_b_gate[i], (0, GLA_KP - GLA_KW)).reshape(1, GLA_KP)
        wo, wup, wdn = w_out[i].astype(BF16), w_up[i].astype(BF16), w_down[i].astype(BF16)
        wpg, wple = w_ple_gate[i].astype(BF16), w_ple[i].astype(BF16)
        proj_w = (row(norm_pre_mix[i]), w_all, row(lower_bounds[i]))
        mlp_w = (row(attn_norm[i]), wo, row(norm_post_mix[i]), row(norm_pre_mlp[i]), wup, wdn,
                 row(norm_post_mlp[i]), wpg, wple)

        (hq, hk, hv, hlf, hgt, gq, gk, gv, gla, ggt, *att_streams, k_t, v_t) = _proj_call(
            hp, *proj_w, rope_p, wg2, bg, kv_p, i, depth, tm_p, tp)
        kv_p = (k_t, v_t)
        sh = lambda a: a.reshape(bp, tp, a.shape[-1])
        oh, s_h = _recur_call(sh(hq), sh(hk), sh(hv), sh(hlf), sh(hgt), row(hgrn_norm[i]), None,
                              HG_HEADS, HG_DK, HG_DV, "hgrn")
        og, s_g = _recur_call(sh(gq), sh(gk), sh(gv), sh(gla), sh(ggt), row(gla_norm[i]), None,
                              GLA_HEADS, GLA_DK, GLA_DV, "gla")
        oa = _prompt_attn_call(att_streams, tp)
        fl = lambda a: a.reshape(bp * tp, a.shape[-1])
        hp = _mlp_call(hp, fl(oh), fl(oa), fl(og), pp3, i, *mlp_w, tm_p)
        hg_p.append(s_h.reshape(bp, HG_HEADS, HG_DK, HG_DV))
        gl_p.append(s_g.reshape(bp, GLA_HEADS, GLA_DK, GLA_DV))

        (hq, hk, hv, hlf, hgt, gq, gk, gv, gla, ggt, aq, k_new, v_new) = _proj_call(
            hs, *proj_w, rope_s, wg2, bg, None, 0, 1, tm_s, None)
        pt = lambda a: jnp.pad(a.reshape(bs, ts, a.shape[-1]), ((0, 0), (0, CHUNK - ts), (0, 0)))
        oh, s_h = _recur_call(pt(hq), pt(hk), pt(hv), pt(hlf), pt(hgt), row(hgrn_norm[i]),
                              state_hgrn[i].reshape(bs, HG_W, HG_DV), HG_HEADS, HG_DK, HG_DV, "hgrn_s")
        og, s_g = _recur_call(pt(gq), pt(gk), pt(gv), pt(gla), pt(ggt), row(gla_norm[i]),
                              state_gla[i].reshape(bs, GLA_KW, GLA_DV), GLA_HEADS, GLA_DK, GLA_DV, "gla_s")
        s3 = lambda a: a.reshape(bs, ts, ATT_W)
        oa, ck_new, cv_new = _sample_attn_call(s3(aq), s3(k_new), s3(v_new), ck_t, cv_t, kv_s, i)
        kv_s = (ck_new, cv_new)
        ut = lambda a: a[:, :ts].reshape(bs * ts, a.shape[-1])
        hs = _mlp_call(hs, ut(oh), oa.reshape(bs * ts, ATT_W), ut(og), ps3, i, *mlp_w, tm_s)
        hg_s.append(s_h.reshape(bs, HG_HEADS, HG_DK, HG_DV))
        gl_s.append(s_g.reshape(bs, GLA_HEADS, GLA_DK, GLA_DV))

    return (hp.reshape(bp, tp, D_MODEL), hs.reshape(bs, ts, D_MODEL),
            jnp.stack(hg_p), jnp.stack(gl_p), from_t(kv_p[0]), from_t(kv_p[1]),
            jnp.stack(hg_s), jnp.stack(gl_s), from_t(kv_s[0]), from_t(kv_s[1]))
```

```python
import functools
import math

import jax
import jax.numpy as jnp
import numpy as np
from jax import lax
from jax.experimental import pallas as pl
from jax.experimental.pallas import tpu as pltpu

F32 = jnp.float32
BF16 = jnp.bfloat16

D_MODEL = 1024
HEAD_DIM = 64
HG_HEADS, HG_DK, HG_DV = 4, 64, 64
ATT_HEADS = 6
GLA_HEADS, GLA_DK, GLA_DV = 6, 32, 64
GLA_GATE_RANK = 16
GLA_TAU = 16.0
D_FF = 4 * D_MODEL
D_PLE = 256
ROPE_THETA = 500000.0
ROT_DIM = HEAD_DIM // 4
DILATED_PATTERNS = ((128, 1), (512, 4), (2048, 16))
MAX_WINDOW = 2048
PAST_LEN = 16384
EPS = 1e-6

LANES = 128
HG_W = HG_HEADS * HG_DK
ATT_W = ATT_HEADS * HEAD_DIM
GLA_KW = GLA_HEADS * GLA_DK
GLA_KP = 256
GLA_VW = GLA_HEADS * GLA_DV

CHUNK = 64
RECUR_GROUP = 8
RECUR_TT = 512
QBLK = 128
NEG = -1e30
EXP_CLAMP = 80.0
Q_SCALE = HEAD_DIM ** -0.5 * math.log2(math.e)
VMEM_LIMIT = 56 * 1024 * 1024

C_HG = 0
C_AT = C_HG + 4 * HG_W
C_GL = C_AT + 3 * ATT_W
GL_Q, GL_K, GL_V, GL_R, GL_LR = 0, 256, 512, 896, 1280
GL_COLS = 1408
W_COLS = C_GL + GL_COLS


def _rms(x):
    return x * lax.rsqrt(jnp.mean(x * x, axis=-1, keepdims=True) + EPS)


def _sigmoid(x):
    return 1.0 / (1.0 + jnp.exp(-x))


def _dot(a, b):
    return jnp.dot(a, b, preferred_element_type=F32)


def _dot_nt(a, b):
    return lax.dot_general(a, b, (((1,), (1,)), ((), ())), preferred_element_type=F32)


def _dot_tn(a, b):
    return lax.dot_general(a, b, (((0,), (0,)), ((), ())), preferred_element_type=F32)


def _iota(shape, d):
    return lax.broadcasted_iota(jnp.int32, shape, d)


def _const_spec(shape):
    nd = len(shape)
    return pl.BlockSpec(shape, lambda *_: (0,) * nd, pipeline_mode=pl.Buffered(1))


def _layer_spec(shape, layer):
    nd = len(shape)
    return pl.BlockSpec((None,) + tuple(shape), lambda *_: (layer,) + (0,) * nd, pipeline_mode=pl.Buffered(1))


def _params(*semantics):
    return pltpu.CompilerParams(dimension_semantics=semantics, vmem_limit_bytes=VMEM_LIMIT)


def _proj_kernel(streams, x_ref, gpre_ref, w_ref, lb_ref, cos_ref, sa_ref, sb_ref, wg2_ref, bg_ref,
                 hq_o, hk_o, hv_o, hlf_o, hgt_o, gq_o, gk_o, gv_o, gla_o, ggt_o, *att_refs):
    xn = (_rms(x_ref[...]) * gpre_ref[...]).astype(BF16)

    y_at = _dot(xn, w_ref[:, C_AT:C_AT + 3 * ATT_W])
    y_hg = _dot(xn, w_ref[:, C_HG:C_HG + 4 * HG_W])
    y_gl = _dot(xn, w_ref[:, C_GL:C_GL + GL_COLS])

    y = y_at
    cos_t, sin_a, sin_b = cos_ref[...], sa_ref[...], sb_ref[...]

    def rope(v):
        return v * cos_t + pltpu.roll(v, LANES - ROT_DIM // 2, 1) * sin_a + pltpu.roll(v, ROT_DIM // 2, 1) * sin_b

    if streams:
        *stream_os, ak_o, av_o, att_scr = att_refs
    else:
        aq_o, ak_o, av_o = att_refs
    for j in range(ATT_W // LANES):
        sl = slice(j * LANES, (j + 1) * LANES)
        q_rot = rope(y[:, sl]) * Q_SCALE
        k_rot = rope(y[:, ATT_W + j * LANES:ATT_W + (j + 1) * LANES])
        v_grp = y[:, 2 * ATT_W + j * LANES:2 * ATT_W + (j + 1) * LANES]
        if streams:
            ak_o[0, 0, sl, :] = k_rot.T
            av_o[0, 0, sl, :] = v_grp.T
            n_grp = ATT_W // LANES
            att_scr[j], att_scr[n_grp + j], att_scr[2 * n_grp + j] = q_rot, k_rot, v_grp
        else:
            aq_o[:, sl] = q_rot
            ak_o[:, sl] = k_rot
            av_o[:, sl] = v_grp
    if streams:
        for s_o in stream_os:
            d, n = s_o.shape[1], s_o.shape[2]
            for r in range(d):
                rows = pl.ds(r, n, stride=d) if d > 1 else slice(None)
                for grp in range(att_scr.shape[0]):
                    s_o[0, r, :, grp * LANES:(grp + 1) * LANES] = att_scr[grp, rows, :].astype(BF16)

    y = y_hg
    lb = lb_ref[...]
    f = lb + (1.0 - lb) * _sigmoid(y[:, HG_W:2 * HG_W])
    hg = y[:, 3 * HG_W:4 * HG_W]
    hq_o[...] = y[:, 0:HG_W].astype(BF16)
    hk_o[...] = (1.0 - f).astype(BF16)
    hv_o[...] = y[:, 2 * HG_W:3 * HG_W].astype(BF16)
    hlf_o[...] = jnp.log(f)
    hgt_o[...] = (hg * _sigmoid(hg)).astype(BF16)

    y = y_gl
    gr = y[:, GL_R:GL_R + GLA_VW]
    z = _dot(y[:, GL_LR:GL_LR + LANES].astype(BF16), wg2_ref[...]) + bg_ref[...]
    log_a = (jnp.minimum(z, 0.0) - jnp.log(1.0 + jnp.exp(-jnp.abs(z)))) * (1.0 / GLA_TAU)
    gq_o[...] = (y[:, GL_Q:GL_Q + GLA_KP] * (GLA_DK ** -0.5)).astype(BF16)
    gk_o[...] = y[:, GL_K:GL_K + GLA_KP].astype(BF16)
    gv_o[...] = y[:, GL_V:GL_V + GLA_VW].astype(BF16)
    gla_o[...] = log_a
    ggt_o[...] = (gr * _sigmoid(gr)).astype(BF16)


def _proj_call(x2d, gpre, w_all, lb, rope_tabs, wg2, bg, kv_prev, w_layer, layer, depth, tm, seq_len):
    n = x2d.shape[0]
    cos_t, sin_a, sin_b = rope_tabs
    tab_blocks = cos_t.shape[0] // tm
    row = lambda w: pl.BlockSpec((tm, w), lambda i: (i, 0))
    tab = pl.BlockSpec((tm, LANES), lambda i: (i % tab_blocks, 0))
    sd = jax.ShapeDtypeStruct
    lspec = lambda *shape: _layer_spec(shape, w_layer)
    in_specs = [row(D_MODEL), lspec(1, D_MODEL), lspec(D_MODEL, W_COLS), lspec(1, HG_W), tab, tab, tab,
                lspec(LANES, GLA_KP), lspec(1, GLA_KP)]
    args = [x2d, gpre, w_all, lb, cos_t, sin_a, sin_b, wg2, bg]
    out_shape = [sd((n, HG_W), BF16), sd((n, HG_W), BF16), sd((n, HG_W), BF16), sd((n, HG_W), F32),
                 sd((n, HG_W), BF16), sd((n, GLA_KP), BF16), sd((n, GLA_KP), BF16),
                 sd((n, GLA_VW), BF16), sd((n, GLA_KP), F32), sd((n, GLA_VW), BF16)]
    out_specs = [row(HG_W)] * 5 + [row(GLA_KP), row(GLA_KP), row(GLA_VW), row(GLA_KP), row(GLA_VW)]
    aliases, scratch = {}, []
    if seq_len is not None:
        nt, bsz = seq_len // tm, n // seq_len
        for _, d in DILATED_PATTERNS:
            out_shape.append(sd((bsz, d, seq_len // d, 3 * ATT_W), BF16))
            out_specs.append(pl.BlockSpec((1, d, tm // d, 3 * ATT_W), lambda i: (i // nt, 0, i % nt, 0)))
        out_shape += [sd((depth, bsz, ATT_W, seq_len), F32)] * 2
        out_specs += [pl.BlockSpec((1, 1, ATT_W, tm), lambda i: (layer, i // nt, 0, i % nt))] * 2
        scratch = [pltpu.VMEM((3 * ATT_W // LANES, tm, LANES), F32)]
        if kv_prev is not None:
            in_specs += [pl.BlockSpec(memory_space=pl.ANY)] * 2
            args += list(kv_prev)
            aliases = {len(args) - 2: len(out_shape) - 2, len(args) - 1: len(out_shape) - 1}
    else:
        out_shape += [sd((n, ATT_W), F32)] * 3
        out_specs += [row(ATT_W)] * 3
    n_in = len(args)

    def body(*refs):
        _proj_kernel(seq_len is not None, *refs[:9], *refs[n_in:])

    return pl.pallas_call(
        body, grid=(n // tm,), in_specs=in_specs, out_specs=out_specs, out_shape=out_shape,
        scratch_shapes=scratch, input_output_aliases=aliases, name="proj", compiler_params=_params("parallel"),
    )(*args)


def _recur_kernel(heads, dk, dv, has_init, *refs):
    if has_init:
        q_ref, k_ref, v_ref, g_ref, gate_ref, nw_ref, s0_ref, o_ref, s_out_ref, s_scr = refs
    else:
        q_ref, k_ref, v_ref, g_ref, gate_ref, nw_ref, o_ref, s_out_ref, s_scr = refs
    group, t_tile, kwp = q_ref.shape
    kw, vw = heads * dk, heads * dv
    per_k = LANES // dk
    per_v = LANES // dv
    t_idx = pl.program_id(1)

    tri = (_iota((CHUNK, CHUNK), 0) >= _iota((CHUNK, CHUNK), 1)).astype(BF16)
    causal = (_iota((per_k * CHUNK, CHUNK), 0) % CHUNK) >= _iota((per_k * CHUNK, CHUNK), 1)
    own_head = ((_iota((per_k * CHUNK, LANES), 0) // CHUNK) == (_iota((per_k * CHUNK, LANES), 1) // dk)).astype(BF16)
    bd_mask = (_iota((vw, kwp), 0) // dv) == (_iota((vw, kwp), 1) // dk)
    pool = jnp.where((_iota((vw, vw), 0) // dv) == (_iota((vw, vw), 1) // dv), 1.0 / dv, 0.0).astype(BF16)
    v_head = _iota((CHUNK, LANES), 1) // dv
    nw = nw_ref[...]

    @pl.when(t_idx == 0)
    def _():
        for g in range(group):
            if has_init:
                s_nat = s0_ref[g]
                if kwp > kw:
                    s_nat = jnp.concatenate([s_nat, jnp.zeros((kwp - kw, dv), F32)], axis=0)
                tiled = jnp.concatenate([s_nat] * heads, axis=1)
                s_scr[g] = jnp.where(bd_mask, tiled.T, 0.0)
            else:
                s_scr[g] = jnp.zeros((vw, kwp), F32)

    def split_dot(a, x, terms):
        acc = None
        for _ in range(terms):
            hi = x.astype(BF16)
            part = _dot(a, hi)
            acc = part if acc is None else acc + part
            x = x - hi.astype(F32)
        return acc

    k_groups = [(kg, min(per_k, heads - kg * per_k)) for kg in range(kwp // LANES) if heads > kg * per_k]

    short = t_tile < CHUNK

    def load(ref, g, rows):
        if not short:
            return ref[g, rows, :]
        x = ref[g].astype(F32)
        return jnp.concatenate([x, jnp.zeros((CHUNK - t_tile, x.shape[1]), F32)], axis=0)

    def step(c, carry):
        rows = pl.ds(pl.multiple_of(c * CHUNK, CHUNK), CHUNK)
        seqs = range(group)
        b = [split_dot(tri, load(g_ref, g, rows), 2) for g in seqs]
        q_in, k_in, q_st, k_st, dec = [], [], [], [], []
        for g in seqs:
            q = load(q_ref, g, rows).astype(F32)
            k = load(k_ref, g, rows).astype(F32)
            b_last = b[g][CHUNK - 1:CHUNK, :]
            mid = 0.5 * b_last
            q_in.append((q * jnp.exp(jnp.minimum(b[g] - mid, EXP_CLAMP))).astype(BF16))
            k_in.append((k * jnp.exp(jnp.minimum(mid - b[g], EXP_CLAMP))).astype(BF16))
            q_st.append((q * jnp.exp(b[g])).astype(BF16))
            k_st.append((k * jnp.exp(b_last - b[g])).astype(BF16))
            dec.append(jnp.exp(b_last))
        a = []
        for g in seqs:
            per_group = []
            for kg, nh in k_groups:
                sl = slice(kg * LANES, (kg + 1) * LANES)
                stack = jnp.concatenate([q_in[g][:, sl]] * nh, axis=0) * own_head[0:nh * CHUNK]
                per_group.append(_dot_nt(stack, k_in[g][:, sl]))
            a.append(per_group)
        s_old = [s_scr[g] for g in seqs]
        o_inter = [_dot_nt(q_st[g], s_old[g].astype(BF16)) for g in seqs]
        v = [load(v_ref, g, rows).astype(BF16) for g in seqs]
        u = [_dot_tn(v[g], k_st[g]) for g in seqs]
        for g in seqs:
            s_scr[g] = s_old[g] * dec[g] + jnp.where(bd_mask, u[g], 0.0)
        o = []
        for g in seqs:
            a_heads = []
            for (kg, nh), ag in zip(k_groups, a[g]):
                am = jnp.where(causal[0:nh * CHUNK], ag, 0.0).astype(BF16)
                a_heads += [am[e * CHUNK:(e + 1) * CHUNK] for e in range(nh)]
            pieces = []
            for j in range(vw // LANES):
                v_grp = v[g][:, j * LANES:(j + 1) * LANES]
                acc = None
                for e in range(per_v):
                    oe = _dot(a_heads[j * per_v + e], v_grp)
                    acc = oe if acc is None else jnp.where(v_head == e, oe, acc)
                pieces.append(acc)
            o.append(o_inter[g] + jnp.concatenate(pieces, axis=1))
        ms = [_dot((o[g] * o[g]).astype(BF16), pool) for g in seqs]
        for g in seqs:
            gate = load(gate_ref, g, rows).astype(F32)
            out = (o[g] * lax.rsqrt(ms[g] + EPS) * nw * gate).astype(o_ref.dtype)
            if short:
                o_ref[g] = out[0:t_tile]
            else:
                o_ref[g, rows, :] = out
        return carry

    lax.fori_loop(0, max(1, t_tile // CHUNK), step, 0)

    @pl.when(t_idx == pl.num_programs(1) - 1)
    def _():
        for g in range(group):
            s_bd = s_scr[g].T
            s_nat = s_bd[:, 0:dv]
            for h in range(1, heads):
                s_nat = s_nat + s_bd[:, h * dv:(h + 1) * dv]
            s_out_ref[g] = s_nat[0:kw]


def _recur_call(q, k, v, g, gate, nw, layer, s0, heads, dk, dv, name):
    bsz, t_len, kwp = q.shape
    kw, vw = heads * dk, heads * dv
    group = math.gcd(RECUR_GROUP, bsz)
    tt = min(RECUR_TT, t_len)
    seq = lambda w: pl.BlockSpec((group, tt, w), lambda b, t: (b, t, 0))
    st = pl.BlockSpec((group, kw, dv), lambda b, t: (b, 0, 0))
    in_specs = [seq(kwp), seq(kwp), seq(vw), seq(kwp), seq(vw), _layer_spec((1, vw), layer)]
    args = [q, k, v, g, gate, nw]
    if s0 is not None:
        in_specs.append(st)
        args.append(s0)
    return pl.pallas_call(
        functools.partial(_recur_kernel, heads, dk, dv, s0 is not None),
        grid=(bsz // group, t_len // tt), in_specs=in_specs, out_specs=[seq(vw), st],
        out_shape=[jax.ShapeDtypeStruct((bsz, t_len, vw), BF16), jax.ShapeDtypeStruct((bsz, kw, dv), F32)],
        scratch_shapes=[pltpu.VMEM((group, vw, kwp), F32)], name=name,
        compiler_params=_params("parallel", "arbitrary"),
    )(*args)


def _log_multiplicity(delta):
    delta = np.asarray(delta, np.int64)
    cnt = np.zeros(delta.shape, np.float64)
    for w, d in DILATED_PATTERNS:
        cnt += (delta >= 0) & (delta <= w) & (delta % d == 0)
    return np.where(cnt > 0, np.log2(np.maximum(cnt, 1.0)), NEG).astype(np.float32)


ATTN_UNITS_PER_GROUP = 16


def _prompt_attn_kernel(*refs):
    n_pat = len(DILATED_PATTERNS)
    qs, ks, vs = refs[0:3 * n_pat:3], refs[1:3 * n_pat:3], refs[2:3 * n_pat:3]
    mask_ref, o_ref = refs[3 * n_pat:3 * n_pat + 2]
    scr = refs[3 * n_pat + 2:]
    nd, md, ld = scr[0::3], scr[1::3], scr[2::3]
    t_len = o_ref.shape[1]
    dils = [d for _, d in DILATED_PATTERNS]
    lo = _iota((QBLK, LANES), 1) < HEAD_DIM

    units = []
    for di, d in enumerate(dils):
        for r in range(d):
            for pb in range(t_len // d // QBLK):
                q0 = pb * QBLK
                k0, klen = (q0, QBLK) if pb == 0 else (q0 - QBLK, 2 * QBLK)
                units.append((di, r, q0, k0, klen))

    eye = (_iota((QBLK, QBLK), 0) == _iota((QBLK, QBLK), 1)).astype(BF16)
    mask_t = mask_ref[...]

    def run(group):
        chains = [(u, e) for u in group for e in (0, 1)]
        s = []
        for (di, r, q0, k0, klen), e in chains:
            q = qs[di][0, r, q0:q0 + QBLK, :]
            q = jnp.where(lo, q, jnp.zeros_like(q)) if e == 0 else jnp.where(lo, jnp.zeros_like(q), q)
            k_aug = jnp.concatenate([ks[di][0, r, k0:k0 + klen, :], mask_t[2 * QBLK - klen:2 * QBLK]], axis=1)
            s.append(_dot_nt(jnp.concatenate([q, eye], axis=1), k_aug))
        m = [jnp.max(x, axis=1, keepdims=True) for x in s]
        p = [jnp.exp2(x - mx) for x, mx in zip(s, m)]
        l = [jnp.sum(x, axis=1, keepdims=True) for x in p]
        num = [_dot(x.astype(BF16), vs[di][0, r, k0:k0 + klen, :])
               for x, ((di, r, _, k0, klen), _) in zip(p, chains)]
        for i, (di, r, q0, _, _) in enumerate(group):
            n_u = jnp.where(lo, num[2 * i], num[2 * i + 1])
            m_u = jnp.where(lo, m[2 * i], m[2 * i + 1])
            l_u = jnp.where(lo, l[2 * i], l[2 * i + 1])
            if di < n_pat - 1:
                blk = (r, slice(q0, q0 + QBLK))
                nd[di][blk], md[di][blk], ld[di][blk] = n_u, m_u, l_u
                continue
            rows = [(r % d, pl.ds(r // d, QBLK, stride=d_max // d)) for d in dils[:-1]]
            ms = [md[dj][rw] for dj, rw in enumerate(rows)] + [m_u]
            top = functools.reduce(jnp.maximum, ms)
            w = [jnp.exp2(x - top) for x in ms]
            acc = sum(wx * nx for wx, nx in zip(w, [nd[dj][rw] for dj, rw in enumerate(rows)] + [n_u]))
            den = sum(wx * lx for wx, lx in zip(w, [ld[dj][rw] for dj, rw in enumerate(rows)] + [l_u]))
            o_ref[0, pl.ds(r, QBLK, stride=d_max), :] = acc * (1.0 / den)

    d_max = dils[-1]
    for i in range(0, len(units), ATTN_UNITS_PER_GROUP):
        run(units[i:i + ATTN_UNITS_PER_GROUP])


def _prompt_attn_call(streams, t_len):
    bsz = streams[0].shape[0]
    dils = [d for _, d in DILATED_PATTERNS]
    assert all(w // d == QBLK for w, d in DILATED_PATTERNS) and t_len == QBLK * dils[-1] == QBLK * max(dils)
    a = np.arange(QBLK)[:, None]
    c = np.arange(2 * QBLK)[None, :]
    mask_t = jnp.asarray(np.where((c >= a) & (c <= a + QBLK), 0.0, NEG).astype(np.float32).T, dtype=BF16)
    n_grp = ATT_W // LANES
    in_specs, args = [], []
    for s_d in streams:
        d, n = s_d.shape[1], s_d.shape[2]
        for part in range(3):
            in_specs.append(pl.BlockSpec((1, d, n, LANES), lambda b, j, part=part: (b, 0, 0, part * n_grp + j)))
            args.append(s_d)
    scratch = []
    for d in dils[:-1]:
        scratch += [pltpu.VMEM((d, t_len // d, LANES), F32)] * 3
    return pl.pallas_call(
        _prompt_attn_kernel, grid=(bsz, n_grp),
        in_specs=in_specs + [_const_spec((2 * QBLK, QBLK))],
        out_specs=pl.BlockSpec((1, t_len, LANES), lambda b, j: (b, 0, j)),
        out_shape=jax.ShapeDtypeStruct((bsz, t_len, ATT_W), F32), name="prompt_attn",
        scratch_shapes=scratch,
        compiler_params=_params("parallel", "parallel"),
    )(*args, mask_t)


def _sample_attn_kernel(with_prev, *refs):
    if with_prev:
        q_ref, kn_ref, vn_ref, ck_ref, cv_ref, bias_ref, _, _, o_ref, ok_ref, ov_ref, kb_scr, vb_scr = refs
    else:
        q_ref, kn_ref, vn_ref, ck_ref, cv_ref, bias_ref, o_ref, ok_ref, ov_ref, kb_scr, vb_scr = refs
    win = ck_ref.shape[3]
    t_new = kn_ref.shape[1]
    tail = _iota((LANES, LANES), 1) >= LANES - t_new
    for c_ref, n_ref, out_ref, scr in ((ck_ref, kn_ref, ok_ref, kb_scr), (cv_ref, vn_ref, ov_ref, vb_scr)):
        new_t = jnp.concatenate([n_ref[0], jnp.zeros((LANES - t_new, ATT_W), F32)], axis=0).T
        for rb in range(ATT_W // LANES):
            rows = slice(rb * LANES, (rb + 1) * LANES)
            old = c_ref[0, 0, rows, :]
            shifted = pltpu.roll(old, win - t_new, 1)
            out_ref[0, 0, rows, 0:win - LANES] = shifted[:, 0:win - LANES]
            out_ref[0, 0, rows, win - LANES:win] = jnp.where(
                tail, pltpu.roll(new_t[rows], LANES - t_new, 1), shifted[:, win - LANES:win])
            scr[rows, 0:win] = old.astype(BF16)
            scr[rows, win:win + LANES] = new_t[rows].astype(BF16)
    q = q_ref[0].astype(F32)
    rows = ATT_HEADS * t_new
    own = (_iota((rows, ATT_W), 0) // t_new) == (_iota((rows, ATT_W), 1) // HEAD_DIM)
    q_stack = jnp.where(own, jnp.concatenate([q] * ATT_HEADS, axis=0), 0.0).astype(BF16)
    s = _dot(q_stack, kb_scr[...]) + bias_ref[...]
    m = jnp.max(s, axis=1, keepdims=True)
    p = jnp.exp2(s - m)
    l = jnp.sum(p, axis=1, keepdims=True)
    o_all = jnp.where(own, _dot_nt(p.astype(BF16), vb_scr[...]) * (1.0 / l), 0.0)
    o = o_all[0:t_new]
    for h in range(1, ATT_HEADS):
        o = o + o_all[h * t_new:(h + 1) * t_new]
    o_ref[0] = o.astype(o_ref.dtype)


def _sample_attn_call(q, k_new, v_new, cache_kt, cache_vt, prev, layer):
    depth, bsz, _, win = cache_kt.shape
    t_new = q.shape[1]
    rows = ATT_HEADS * t_new
    tq = np.arange(rows)[:, None] % t_new
    n = np.arange(win + LANES)[None, :]
    bias = np.where(n < win + t_new, _log_multiplicity(win + tq - n), NEG).astype(np.float32)
    new = pl.BlockSpec((1, t_new, ATT_W), lambda b: (b, 0, 0))
    cache = pl.BlockSpec((1, 1, ATT_W, win), lambda b: (layer, b, 0, 0))
    in_specs = [new, new, new, cache, cache, _const_spec((rows, win + LANES))]
    args = [q, k_new, v_new, cache_kt, cache_vt, jnp.asarray(bias)]
    aliases = {}
    if prev is not None:
        in_specs += [pl.BlockSpec(memory_space=pl.ANY)] * 2
        args += list(prev)
        aliases = {6: 1, 7: 2}
    sd = jax.ShapeDtypeStruct
    return pl.pallas_call(
        functools.partial(_sample_attn_kernel, prev is not None), grid=(bsz,),
        in_specs=in_specs, out_specs=[new, cache, cache],
        out_shape=[sd((bsz, t_new, ATT_W), BF16), sd(cache_kt.shape, F32), sd(cache_vt.shape, F32)],
        scratch_shapes=[pltpu.VMEM((ATT_W, win + LANES), BF16)] * 2,
        input_output_aliases=aliases, name="sample_attn", compiler_params=_params("parallel"),
    )(*args)


FF_CHUNK = 1024


def _mlp_kernel(h_ref, oh_ref, oa_ref, og_ref, p_ref, an_ref, wo_ref, npm_ref, npre_ref, wup_ref, wdn_ref,
                npost_ref, wpg_ref, wple_ref, out_ref):
    half = h_ref.shape[0] // 2
    halves = [slice(0, half), slice(half, 2 * half)]
    oa = [(_rms(oa_ref[s, :].astype(F32)) * an_ref[...]).astype(BF16) for s in halves]
    mix = [_dot(oh_ref[s, :], wo_ref[0:HG_W, :]) + _dot(oa[i], wo_ref[HG_W:HG_W + ATT_W, :])
           + _dot(og_ref[s, :], wo_ref[HG_W + ATT_W:HG_W + ATT_W + GLA_VW, :]) for i, s in enumerate(halves)]
    ple = [_dot(p_ref[s, :].astype(BF16), wple_ref[...]) for s in halves]
    h = [h_ref[s, :] + _rms(mix[i]) * npm_ref[...] for i, s in enumerate(halves)]
    xn = [(_rms(x) * npre_ref[...]).astype(BF16) for x in h]
    acc = [None, None]
    for c in range(D_FF // FF_CHUNK):
        cols = slice(c * FF_CHUNK, (c + 1) * FF_CHUNK)
        u = [jnp.maximum(_dot(x, wup_ref[:, cols]), 0.0) for x in xn]
        part = [_dot((x * x).astype(BF16), wdn_ref[cols, :]) for x in u]
        acc = [p if a is None else a + p for a, p in zip(acc, part)]
    h = [x + _rms(a) * npost_ref[...] for x, a in zip(h, acc)]
    gate = [_sigmoid(_dot(x.astype(BF16), wpg_ref[...])) for x in h]
    for i, s in enumerate(halves):
        out_ref[s, :] = h[i] + gate[i] * ple[i]


def _mlp_call(h2d, oh, oa, og, p3d, layer, an, wo, npm, npre, wup, wdn, npost, wpg, wple, tm):
    n = h2d.shape[0]
    row = lambda w: pl.BlockSpec((tm, w), lambda i: (i, 0))
    lspec = lambda *shape: _layer_spec(shape, layer)
    vec = lspec(1, D_MODEL)
    return pl.pallas_call(
        _mlp_kernel, grid=(n // tm,),
        in_specs=[row(D_MODEL), row(HG_W), row(ATT_W), row(GLA_VW),
                  pl.BlockSpec((None, tm, D_PLE), lambda i: (layer, i, 0)), lspec(1, ATT_W),
                  lspec(D_MODEL, D_MODEL), vec, vec, lspec(D_MODEL, D_FF), lspec(D_FF, D_MODEL), vec,
                  lspec(D_MODEL, D_MODEL), lspec(D_PLE, D_MODEL)],
        out_specs=row(D_MODEL), out_shape=jax.ShapeDtypeStruct((n, D_MODEL), F32), name="mlp",
        compiler_params=_params("parallel"),
    )(h2d, oh, oa, og, p3d, an, wo, npm, npre, wup, wdn, npost, wpg, wple)


def _pack_w_in(w_in):
    splits = np.cumsum([0, HG_W, HG_W, HG_W, HG_W, ATT_W, ATT_W, ATT_W, GLA_KW, GLA_KW, GLA_VW,
                        GLA_GATE_RANK, GLA_VW])
    col = lambda i: w_in[..., splits[i]:splits[i + 1]]
    padc = lambda a, w: jnp.pad(a, ((0, 0), (0, 0), (0, w - a.shape[-1])))
    parts = [col(0), col(1), col(2), col(3), col(4), col(5), col(6),
             padc(col(7), GLA_KP), padc(col(8), GLA_KP), col(9), col(11), padc(col(10), LANES)]
    return jnp.concatenate(parts, axis=-1).astype(BF16)


def _rope_tables(pos):
    half = ROT_DIM // 2
    inv = jnp.exp(-math.log(ROPE_THETA) * jnp.arange(half, dtype=F32) * (2.0 / ROT_DIM))
    ang = pos[:, None] * inv[None, :]
    cos, sin = jnp.cos(ang), jnp.sin(ang)
    d = np.arange(LANES) % HEAD_DIM
    first, second = d < half, (d >= half) & (d < ROT_DIM)
    idx = np.where(second, d - half, np.where(first, d, 0))
    cos_t = jnp.where(first | second, cos[:, idx], 1.0)
    sin_a = jnp.where(first, -sin[:, idx], 0.0)
    sin_b = jnp.where(second, sin[:, idx], 0.0)
    return cos_t, sin_a, sin_b


def kernel(x_prompt, x_sample, state_hgrn, state_gla, cache_k, cache_v, p_prompt, p_sample, norm_pre_mix, w_in, hgrn_lb, hgrn_norm, attn_norm, gla_w_gate2, gla_b_gate, gla_norm, w_out, norm_post_mix, norm_pre_mlp, w_up, w_down, norm_post_mlp, w_ple_gate, w_ple):
    depth = w_in.shape[0]
    bp, tp, _ = x_prompt.shape
    bs, ts, _ = x_sample.shape
    win = cache_k.shape[2]

    lb_cum = jnp.cumsum(jax.nn.softmax(hgrn_lb.astype(F32), axis=0), axis=0)
    lower_bounds = lb_cum - lb_cum[0:1]

    tm_p = min(512, tp)
    tm_s = bs * ts
    rope_p = _rope_tables(jnp.arange(tp, dtype=F32))
    rope_s = tuple(jnp.tile(t, (bs, 1)) for t in _rope_tables(jnp.arange(ts, dtype=F32) + PAST_LEN))
    to_t = lambda a: jnp.transpose(a, (0, 1, 3, 4, 2)).reshape(depth, a.shape[1], ATT_W, a.shape[2])
    from_t = lambda a: jnp.transpose(a.reshape(depth, a.shape[1], ATT_HEADS, HEAD_DIM, a.shape[3]), (0, 1, 4, 2, 3))
    ck_t, cv_t = to_t(cache_k), to_t(cache_v)
    pp3 = p_prompt.reshape(depth, bp * tp, D_PLE)
    ps3 = p_sample.reshape(depth, bs * ts, D_PLE)

    hp = x_prompt.reshape(bp * tp, D_MODEL)
    hs = x_sample.reshape(bs * ts, D_MODEL)
    kv_p = None
    kv_s = None
    hg_p, gl_p, hg_s, gl_s = [], [], [], []
    row = lambda a: a.reshape(depth, 1, -1)
    w_all = _pack_w_in(w_in)
    wg2 = jnp.pad(gla_w_gate2, ((0, 0), (0, LANES - GLA_GATE_RANK), (0, GLA_KP - GLA_KW))).astype(BF16)
    bg = row(jnp.pad(gla_b_gate, ((0, 0), (0, GLA_KP - GLA_KW))))
    proj_w = (row(norm_pre_mix), w_all, row(lower_bounds))
    mlp_w = (row(attn_norm), w_out.astype(BF16), row(norm_post_mix), row(norm_pre_mlp), w_up.astype(BF16),
             w_down.astype(BF16), row(norm_post_mlp), w_ple_gate.astype(BF16), w_ple.astype(BF16))
    hgrn_nw, gla_nw = row(hgrn_norm), row(gla_norm)
    for i in range(depth):
        (hq, hk, hv, hlf, hgt, gq, gk, gv, gla, ggt, *att_streams, k_t, v_t) = _proj_call(
            hp, *proj_w, rope_p, wg2, bg, kv_p, i, i, depth, tm_p, tp)
        kv_p = (k_t, v_t)
        sh = lambda a: a.reshape(bp, tp, a.shape[-1])
        oh, s_h = _recur_call(sh(hq), sh(hk), sh(hv), sh(hlf), sh(hgt), hgrn_nw, i, None,
                              HG_HEADS, HG_DK, HG_DV, "hgrn")
        og, s_g = _recur_call(sh(gq), sh(gk), sh(gv), sh(gla), sh(ggt), gla_nw, i, None,
                              GLA_HEADS, GLA_DK, GLA_DV, "gla")
        oa = _prompt_attn_call(att_streams, tp)
        fl = lambda a: a.reshape(bp * tp, a.shape[-1])
        hp = _mlp_call(hp, fl(oh), fl(oa), fl(og), pp3, i, *mlp_w, tm_p)
        hg_p.append(s_h.reshape(bp, HG_HEADS, HG_DK, HG_DV))
        gl_p.append(s_g.reshape(bp, GLA_HEADS, GLA_DK, GLA_DV))

        (hq, hk, hv, hlf, hgt, gq, gk, gv, gla, ggt, aq, k_new, v_new) = _proj_call(
            hs, *proj_w, rope_s, wg2, bg, None, i, 0, 1, tm_s, None)
        pt = lambda a: a.reshape(bs, ts, a.shape[-1])
        oh, s_h = _recur_call(pt(hq), pt(hk), pt(hv), pt(hlf), pt(hgt), hgrn_nw, i,
                              state_hgrn[i].reshape(bs, HG_W, HG_DV), HG_HEADS, HG_DK, HG_DV, "hgrn_s")
        og, s_g = _recur_call(pt(gq), pt(gk), pt(gv), pt(gla), pt(ggt), gla_nw, i,
                              state_gla[i].reshape(bs, GLA_KW, GLA_DV), GLA_HEADS, GLA_DK, GLA_DV, "gla_s")
        s3 = lambda a: a.reshape(bs, ts, ATT_W)
        oa, ck_new, cv_new = _sample_attn_call(s3(aq), s3(k_new), s3(v_new), ck_t, cv_t, kv_s, i)
        kv_s = (ck_new, cv_new)
        ut = lambda a: a.reshape(bs * ts, a.shape[-1])
        hs = _mlp_call(hs, ut(oh), oa.reshape(bs * ts, ATT_W), ut(og), ps3, i, *mlp_w, tm_s)
        hg_s.append(s_h.reshape(bs, HG_HEADS, HG_DK, HG_DV))
        gl_s.append(s_g.reshape(bs, GLA_HEADS, GLA_DK, GLA_DV))

    return (hp.reshape(bp, tp, D_MODEL), hs.reshape(bs, ts, D_MODEL),
            jnp.stack(hg_p), jnp.stack(gl_p), from_t(kv_p[0]), from_t(kv_p[1]),
            jnp.stack(hg_s), jnp.stack(gl_s), from_t(kv_s[0]), from_t(kv_s[1]))
```

```python
import functools
import math

import jax
import jax.numpy as jnp
import numpy as np
from jax import lax
from jax.experimental import pallas as pl
from jax.experimental.pallas import tpu as pltpu

F32 = jnp.float32
BF16 = jnp.bfloat16

D_MODEL = 1024
HEAD_DIM = 64
HG_HEADS, HG_DK, HG_DV = 4, 64, 64
ATT_HEADS = 6
GLA_HEADS, GLA_DK, GLA_DV = 6, 32, 64
GLA_GATE_RANK = 16
GLA_TAU = 16.0
D_FF = 4 * D_MODEL
D_PLE = 256
ROPE_THETA = 500000.0
ROT_DIM = HEAD_DIM // 4
DILATED_PATTERNS = ((128, 1), (512, 4), (2048, 16))
MAX_WINDOW = 2048
PAST_LEN = 16384
EPS = 1e-6

LANES = 128
HG_W = HG_HEADS * HG_DK
ATT_W = ATT_HEADS * HEAD_DIM
GLA_KW = GLA_HEADS * GLA_DK
GLA_KP = 256
GLA_VW = GLA_HEADS * GLA_DV

CHUNK = 64
HALF = CHUNK // 2
RECUR_GROUP = 8
RECUR_TT = 512
QBLK = 128
NEG = -1e30
LOG2_E = math.log2(math.e)
EXP2_CLAMP = 115.0
Q_SCALE = HEAD_DIM ** -0.5 * LOG2_E
VMEM_LIMIT = 56 * 1024 * 1024

C_HG = 0
C_AT = C_HG + 4 * HG_W
C_GL = C_AT + 3 * ATT_W
GL_Q, GL_K, GL_V, GL_R, GL_LR = 0, 256, 512, 896, 1280
GL_COLS = 1408
W_COLS = C_GL + GL_COLS


def _rms(x):
    return x * lax.rsqrt(jnp.mean(x * x, axis=-1, keepdims=True) + EPS)


def _sigmoid(x):
    return 1.0 / (1.0 + jnp.exp(-x))


def _dot(a, b):
    return jnp.dot(a, b, preferred_element_type=F32)


def _dot_nt(a, b):
    return lax.dot_general(a, b, (((1,), (1,)), ((), ())), preferred_element_type=F32)


def _dot_tn(a, b):
    return lax.dot_general(a, b, (((0,), (0,)), ((), ())), preferred_element_type=F32)


def _iota(shape, d):
    return lax.broadcasted_iota(jnp.int32, shape, d)


def _const_spec(shape):
    nd = len(shape)
    return pl.BlockSpec(shape, lambda *_: (0,) * nd, pipeline_mode=pl.Buffered(1))


def _layer_spec(shape, layer):
    nd = len(shape)
    return pl.BlockSpec((None,) + tuple(shape), lambda *_: (layer,) + (0,) * nd, pipeline_mode=pl.Buffered(1))


def _params(*semantics):
    return pltpu.CompilerParams(dimension_semantics=semantics, vmem_limit_bytes=VMEM_LIMIT)


def _proj_kernel(streams, x_ref, gpre_ref, w_ref, lb_ref, cos_ref, sa_ref, sb_ref, wg2_ref, bg_ref,
                 hq_o, hk_o, hv_o, hlf_o, hgt_o, gq_o, gk_o, gv_o, gla_o, ggt_o, *att_refs):
    xn = (_rms(x_ref[...]) * gpre_ref[...]).astype(BF16)

    y_at = _dot(xn, w_ref[:, C_AT:C_AT + 3 * ATT_W])
    y_hg = _dot(xn, w_ref[:, C_HG:C_HG + 4 * HG_W])
    y_gl = _dot(xn, w_ref[:, C_GL:C_GL + GL_COLS])

    y = y_at
    cos_t, sin_a, sin_b = cos_ref[...], sa_ref[...], sb_ref[...]

    def rope(v):
        return v * cos_t + pltpu.roll(v, LANES - ROT_DIM // 2, 1) * sin_a + pltpu.roll(v, ROT_DIM // 2, 1) * sin_b

    if streams:
        *stream_os, ak_o, av_o, att_scr = att_refs
    else:
        aq_o, ak_o, av_o = att_refs
    for j in range(ATT_W // LANES):
        sl = slice(j * LANES, (j + 1) * LANES)
        q_rot = rope(y[:, sl]) * Q_SCALE
        k_rot = rope(y[:, ATT_W + j * LANES:ATT_W + (j + 1) * LANES])
        v_grp = y[:, 2 * ATT_W + j * LANES:2 * ATT_W + (j + 1) * LANES]
        if streams:
            ak_o[0, 0, sl, :] = k_rot.T
            av_o[0, 0, sl, :] = v_grp.T
            n_grp = ATT_W // LANES
            att_scr[j], att_scr[n_grp + j], att_scr[2 * n_grp + j] = q_rot, k_rot, v_grp
        else:
            aq_o[:, sl] = q_rot
            ak_o[:, sl] = k_rot
            av_o[:, sl] = v_grp
    if streams:
        for s_o in stream_os:
            d, n = s_o.shape[1], s_o.shape[2]
            for r in range(d):
                rows = pl.ds(r, n, stride=d) if d > 1 else slice(None)
                for grp in range(att_scr.shape[0]):
                    s_o[0, r, :, grp * LANES:(grp + 1) * LANES] = att_scr[grp, rows, :].astype(BF16)

    y = y_hg
    lb = lb_ref[...]
    f = lb + (1.0 - lb) * _sigmoid(y[:, HG_W:2 * HG_W])
    hg = y[:, 3 * HG_W:4 * HG_W]
    hq_o[...] = y[:, 0:HG_W].astype(BF16)
    hk_o[...] = (1.0 - f).astype(BF16)
    hv_o[...] = y[:, 2 * HG_W:3 * HG_W].astype(BF16)
    hlf_o[...] = jnp.log2(f)
    hgt_o[...] = (hg * _sigmoid(hg)).astype(BF16)

    y = y_gl
    gr = y[:, GL_R:GL_R + GLA_VW]
    z = _dot(y[:, GL_LR:GL_LR + LANES].astype(BF16), wg2_ref[...]) + bg_ref[...]
    log_a = (jnp.minimum(z, 0.0) - jnp.log(1.0 + jnp.exp(-jnp.abs(z)))) * (LOG2_E / GLA_TAU)
    gq_o[...] = (y[:, GL_Q:GL_Q + GLA_KP] * (GLA_DK ** -0.5)).astype(BF16)
    gk_o[...] = y[:, GL_K:GL_K + GLA_KP].astype(BF16)
    gv_o[...] = y[:, GL_V:GL_V + GLA_VW].astype(BF16)
    gla_o[...] = log_a
    ggt_o[...] = (gr * _sigmoid(gr)).astype(BF16)


def _proj_call(x2d, gpre, w_all, lb, rope_tabs, wg2, bg, kv_prev, w_layer, layer, depth, tm, seq_len):
    n = x2d.shape[0]
    cos_t, sin_a, sin_b = rope_tabs
    tab_blocks = cos_t.shape[0] // tm
    row = lambda w: pl.BlockSpec((tm, w), lambda i: (i, 0))
    tab = pl.BlockSpec((tm, LANES), lambda i: (i % tab_blocks, 0))
    sd = jax.ShapeDtypeStruct
    lspec = lambda *shape: _layer_spec(shape, w_layer)
    in_specs = [row(D_MODEL), lspec(1, D_MODEL), lspec(D_MODEL, W_COLS), lspec(1, HG_W), tab, tab, tab,
                lspec(LANES, GLA_KP), lspec(1, GLA_KP)]
    args = [x2d, gpre, w_all, lb, cos_t, sin_a, sin_b, wg2, bg]
    out_shape = [sd((n, HG_W), BF16), sd((n, HG_W), BF16), sd((n, HG_W), BF16), sd((n, HG_W), F32),
                 sd((n, HG_W), BF16), sd((n, GLA_KP), BF16), sd((n, GLA_KP), BF16),
                 sd((n, GLA_VW), BF16), sd((n, GLA_KP), F32), sd((n, GLA_VW), BF16)]
    out_specs = [row(HG_W)] * 5 + [row(GLA_KP), row(GLA_KP), row(GLA_VW), row(GLA_KP), row(GLA_VW)]
    aliases, scratch = {}, []
    if seq_len is not None:
        nt, bsz = seq_len // tm, n // seq_len
        for _, d in DILATED_PATTERNS:
            out_shape.append(sd((bsz, d, seq_len // d, 3 * ATT_W), BF16))
            out_specs.append(pl.BlockSpec((1, d, tm // d, 3 * ATT_W), lambda i: (i // nt, 0, i % nt, 0)))
        out_shape += [sd((depth, bsz, ATT_W, seq_len), F32)] * 2
        out_specs += [pl.BlockSpec((1, 1, ATT_W, tm), lambda i: (layer, i // nt, 0, i % nt))] * 2
        scratch = [pltpu.VMEM((3 * ATT_W // LANES, tm, LANES), F32)]
        if kv_prev is not None:
            in_specs += [pl.BlockSpec(memory_space=pl.ANY)] * 2
            args += list(kv_prev)
            aliases = {len(args) - 2: len(out_shape) - 2, len(args) - 1: len(out_shape) - 1}
    else:
        out_shape += [sd((n, ATT_W), F32)] * 3
        out_specs += [row(ATT_W)] * 3
    n_in = len(args)

    def body(*refs):
        _proj_kernel(seq_len is not None, *refs[:9], *refs[n_in:])

    return pl.pallas_call(
        body, grid=(n // tm,), in_specs=in_specs, out_specs=out_specs, out_shape=out_shape,
        scratch_shapes=scratch, input_output_aliases=aliases, name="proj", compiler_params=_params("parallel"),
    )(*args)


def _recur_kernel(heads, dk, dv, has_init, *refs):
    if has_init:
        q_ref, k_ref, v_ref, g_ref, gate_ref, nw_ref, s0_ref, o_ref, s_out_ref, s_scr = refs
    else:
        q_ref, k_ref, v_ref, g_ref, gate_ref, nw_ref, o_ref, s_out_ref, s_scr = refs
    group, t_tile, kwp = q_ref.shape
    kw, vw = heads * dk, heads * dv
    per_k = LANES // dk
    per_v = LANES // dv
    t_idx = pl.program_id(1)

    tri = (_iota((CHUNK, CHUNK), 0) >= _iota((CHUNK, CHUNK), 1)).astype(BF16)
    causal = [(_iota((per_k * HALF, CHUNK), 0) % HALF) + h * HALF >= _iota((per_k * HALF, CHUNK), 1)
              for h in range(2)]
    own_head = ((_iota((per_k * HALF, LANES), 0) // HALF) == (_iota((per_k * HALF, LANES), 1) // dk)).astype(BF16)
    bd_mask = (_iota((vw, kwp), 0) // dv) == (_iota((vw, kwp), 1) // dk)
    pool = jnp.where((_iota((vw, vw), 0) // dv) == (_iota((vw, vw), 1) // dv), 1.0 / dv, 0.0).astype(BF16)
    v_head = _iota((CHUNK, LANES), 1) // dv
    nw = nw_ref[...]

    @pl.when(t_idx == 0)
    def _():
        for g in range(group):
            if has_init:
                s_nat = s0_ref[g]
                if kwp > kw:
                    s_nat = jnp.concatenate([s_nat, jnp.zeros((kwp - kw, dv), F32)], axis=0)
                tiled = jnp.concatenate([s_nat] * heads, axis=1)
                s_scr[g] = jnp.where(bd_mask, tiled.T, 0.0)
            else:
                s_scr[g] = jnp.zeros((vw, kwp), F32)

    def split_dot(a, x, terms):
        acc = None
        for _ in range(terms):
            hi = x.astype(BF16)
            part = _dot(a, hi)
            acc = part if acc is None else acc + part
            x = x - hi.astype(F32)
        return acc

    k_groups = [(kg, min(per_k, heads - kg * per_k)) for kg in range(kwp // LANES) if heads > kg * per_k]

    short = t_tile < CHUNK

    def load(ref, g, rows):
        if not short:
            return ref[g, rows, :]
        x = ref[g].astype(F32)
        return jnp.concatenate([x, jnp.zeros((CHUNK - t_tile, x.shape[1]), F32)], axis=0)

    def step(c, carry):
        rows = pl.ds(pl.multiple_of(c * CHUNK, CHUNK), CHUNK)
        seqs = range(group)
        b = [split_dot(tri, load(g_ref, g, rows), 2) for g in seqs]
        q_in, k_in, q_st, k_st, dec = [], [], [], [], []
        for g in seqs:
            q = load(q_ref, g, rows).astype(F32)
            k = load(k_ref, g, rows).astype(F32)
            b_half, b_last = b[g][HALF - 1:HALF, :], b[g][CHUNK - 1:CHUNK, :]
            refs_g = (0.5 * b_half, 0.5 * (b_half + b_last))
            q_in.append([(q[h * HALF:(h + 1) * HALF] * jnp.exp2(jnp.minimum(
                b[g][h * HALF:(h + 1) * HALF] - refs_g[h], EXP2_CLAMP))).astype(BF16) for h in range(2)])
            k_in.append([(k * jnp.exp2(jnp.minimum(r - b[g], EXP2_CLAMP))).astype(BF16) for r in refs_g])
            q_st.append((q * jnp.exp2(b[g])).astype(BF16))
            k_st.append((k * jnp.exp2(b_last - b[g])).astype(BF16))
            dec.append(jnp.exp2(b_last))
        a = []
        for g in seqs:
            per_group = []
            for kg, nh in k_groups:
                sl = slice(kg * LANES, (kg + 1) * LANES)
                halves = []
                for h in range(2):
                    stack = jnp.concatenate([q_in[g][h][:, sl]] * nh, axis=0) * own_head[0:nh * HALF]
                    halves.append(_dot_nt(stack, k_in[g][h][:, sl]))
                per_group.append(halves)
            a.append(per_group)
        s_old = [s_scr[g] for g in seqs]
        o_inter = [_dot_nt(q_st[g], s_old[g].astype(BF16)) for g in seqs]
        v = [load(v_ref, g, rows).astype(BF16) for g in seqs]
        u = [_dot_tn(v[g], k_st[g]) for g in seqs]
        for g in seqs:
            s_scr[g] = s_old[g] * dec[g] + jnp.where(bd_mask, u[g], 0.0)
        o = []
        for g in seqs:
            a_heads = []
            for (kg, nh), ag in zip(k_groups, a[g]):
                am = [jnp.where(causal[h][0:nh * HALF], ag[h], 0.0).astype(BF16) for h in range(2)]
                a_heads += [jnp.concatenate([am[h][e * HALF:(e + 1) * HALF] for h in range(2)], axis=0)
                            for e in range(nh)]
            pieces = []
            for j in range(vw // LANES):
                v_grp = v[g][:, j * LANES:(j + 1) * LANES]
                acc = None
                for e in range(per_v):
                    oe = _dot(a_heads[j * per_v + e], v_grp)
                    acc = oe if acc is None else jnp.where(v_head == e, oe, acc)
                pieces.append(acc)
            o.append(o_inter[g] + jnp.concatenate(pieces, axis=1))
        ms_all = _dot(jnp.concatenate([(o[g] * o[g]).astype(BF16) for g in seqs], axis=0), pool)
        ms = [ms_all[g * CHUNK:(g + 1) * CHUNK] for g in seqs]
        for g in seqs:
            gate = load(gate_ref, g, rows).astype(F32)
            out = (o[g] * lax.rsqrt(ms[g] + EPS) * nw * gate).astype(o_ref.dtype)
            if short:
                o_ref[g] = out[0:t_tile]
            else:
                o_ref[g, rows, :] = out
        return carry

    lax.fori_loop(0, max(1, t_tile // CHUNK), step, 0)

    @pl.when(t_idx == pl.num_programs(1) - 1)
    def _():
        for g in range(group):
            s_bd = s_scr[g].T
            s_nat = s_bd[:, 0:dv]
            for h in range(1, heads):
                s_nat = s_nat + s_bd[:, h * dv:(h + 1) * dv]
            s_out_ref[g] = s_nat[0:kw]


def _recur_call(q, k, v, g, gate, nw, layer, s0, heads, dk, dv, name):
    bsz, t_len, kwp = q.shape
    kw, vw = heads * dk, heads * dv
    group = math.gcd(RECUR_GROUP, bsz)
    tt = min(RECUR_TT, t_len)
    seq = lambda w: pl.BlockSpec((group, tt, w), lambda b, t: (b, t, 0))
    st = pl.BlockSpec((group, kw, dv), lambda b, t: (b, 0, 0))
    in_specs = [seq(kwp), seq(kwp), seq(vw), seq(kwp), seq(vw), _layer_spec((1, vw), layer)]
    args = [q, k, v, g, gate, nw]
    if s0 is not None:
        in_specs.append(st)
        args.append(s0)
    return pl.pallas_call(
        functools.partial(_recur_kernel, heads, dk, dv, s0 is not None),
        grid=(bsz // group, t_len // tt), in_specs=in_specs, out_specs=[seq(vw), st],
        out_shape=[jax.ShapeDtypeStruct((bsz, t_len, vw), BF16), jax.ShapeDtypeStruct((bsz, kw, dv), F32)],
        scratch_shapes=[pltpu.VMEM((group, vw, kwp), F32)], name=name,
        compiler_params=_params("parallel", "arbitrary"),
    )(*args)


def _log_multiplicity(delta):
    delta = np.asarray(delta, np.int64)
    cnt = np.zeros(delta.shape, np.float64)
    for w, d in DILATED_PATTERNS:
        cnt += (delta >= 0) & (delta <= w) & (delta % d == 0)
    return np.where(cnt > 0, np.log2(np.maximum(cnt, 1.0)), NEG).astype(np.float32)


ATTN_UNITS_PER_GROUP = 16


def _prompt_attn_kernel(*refs):
    n_pat = len(DILATED_PATTERNS)
    qs, ks, vs = refs[0:3 * n_pat:3], refs[1:3 * n_pat:3], refs[2:3 * n_pat:3]
    mask_ref, o_ref = refs[3 * n_pat:3 * n_pat + 2]
    scr = refs[3 * n_pat + 2:]
    nd, md, ld = scr[0::3], scr[1::3], scr[2::3]
    t_len = o_ref.shape[1]
    dils = [d for _, d in DILATED_PATTERNS]
    lo = _iota((QBLK, LANES), 1) < HEAD_DIM

    units = []
    for di, d in enumerate(dils):
        for r in range(d):
            for pb in range(t_len // d // QBLK):
                q0 = pb * QBLK
                k0, klen = (q0, QBLK) if pb == 0 else (q0 - QBLK, 2 * QBLK)
                units.append((di, r, q0, k0, klen))

    eye = (_iota((QBLK, QBLK), 0) == _iota((QBLK, QBLK), 1)).astype(BF16)
    mask_t = mask_ref[...]

    def run(group):
        chains = [(u, e) for u in group for e in (0, 1)]
        s = []
        for (di, r, q0, k0, klen), e in chains:
            q = qs[di][0, r, q0:q0 + QBLK, :]
            q = jnp.where(lo, q, jnp.zeros_like(q)) if e == 0 else jnp.where(lo, jnp.zeros_like(q), q)
            k_aug = jnp.concatenate([ks[di][0, r, k0:k0 + klen, :], mask_t[2 * QBLK - klen:2 * QBLK]], axis=1)
            s.append(_dot_nt(jnp.concatenate([q, eye], axis=1), k_aug))
        m = [jnp.max(x, axis=1, keepdims=True) for x in s]
        p = [jnp.exp2(x - mx) for x, mx in zip(s, m)]
        l = [jnp.sum(x, axis=1, keepdims=True) for x in p]
        num = [_dot(x.astype(BF16), vs[di][0, r, k0:k0 + klen, :])
               for x, ((di, r, _, k0, klen), _) in zip(p, chains)]
        for i, (di, r, q0, _, _) in enumerate(group):
            n_u = jnp.where(lo, num[2 * i], num[2 * i + 1])
            m_u = jnp.where(lo, m[2 * i], m[2 * i + 1])
            l_u = jnp.where(lo, l[2 * i], l[2 * i + 1])
            if di < n_pat - 1:
                blk = (r, slice(q0, q0 + QBLK))
                nd[di][blk], md[di][blk], ld[di][blk] = n_u, m_u, l_u
                continue
            rows = [(r % d, pl.ds(r // d, QBLK, stride=d_max // d)) for d in dils[:-1]]
            ms = [md[dj][rw] for dj, rw in enumerate(rows)] + [m_u]
            top = functools.reduce(jnp.maximum, ms)
            w = [jnp.exp2(x - top) for x in ms]
            acc = sum(wx * nx for wx, nx in zip(w, [nd[dj][rw] for dj, rw in enumerate(rows)] + [n_u]))
            den = sum(wx * lx for wx, lx in zip(w, [ld[dj][rw] for dj, rw in enumerate(rows)] + [l_u]))
            o_ref[0, pl.ds(r, QBLK, stride=d_max), :] = acc * (1.0 / den)

    d_max = dils[-1]
    for i in range(0, len(units), ATTN_UNITS_PER_GROUP):
        run(units[i:i + ATTN_UNITS_PER_GROUP])


def _prompt_attn_call(streams, t_len):
    bsz = streams[0].shape[0]
    dils = [d for _, d in DILATED_PATTERNS]
    assert all(w // d == QBLK for w, d in DILATED_PATTERNS) and t_len == QBLK * dils[-1] == QBLK * max(dils)
    a = np.arange(QBLK)[:, None]
    c = np.arange(2 * QBLK)[None, :]
    mask_t = jnp.asarray(np.where((c >= a) & (c <= a + QBLK), 0.0, NEG).astype(np.float32).T, dtype=BF16)
    n_grp = ATT_W // LANES
    in_specs, args = [], []
    for s_d in streams:
        d, n = s_d.shape[1], s_d.shape[2]
        for part in range(3):
            in_specs.append(pl.BlockSpec((1, d, n, LANES), lambda b, j, part=part: (b, 0, 0, part * n_grp + j)))
            args.append(s_d)
    scratch = []
    for d in dils[:-1]:
        scratch += [pltpu.VMEM((d, t_len // d, LANES), F32)] * 3
    return pl.pallas_call(
        _prompt_attn_kernel, grid=(bsz, n_grp),
        in_specs=in_specs + [_const_spec((2 * QBLK, QBLK))],
        out_specs=pl.BlockSpec((1, t_len, LANES), lambda b, j: (b, 0, j)),
        out_shape=jax.ShapeDtypeStruct((bsz, t_len, ATT_W), F32), name="prompt_attn",
        scratch_shapes=scratch,
        compiler_params=_params("parallel", "parallel"),
    )(*args, mask_t)


def _sample_attn_kernel(with_prev, *refs):
    if with_prev:
        q_ref, kn_ref, vn_ref, ck_ref, cv_ref, bias_ref, _, _, o_ref, ok_ref, ov_ref, kb_scr, vb_scr = refs
    else:
        q_ref, kn_ref, vn_ref, ck_ref, cv_ref, bias_ref, o_ref, ok_ref, ov_ref, kb_scr, vb_scr = refs
    win = ck_ref.shape[3]
    t_new = kn_ref.shape[1]
    tail = _iota((LANES, LANES), 1) >= LANES - t_new
    for c_ref, n_ref, out_ref, scr in ((ck_ref, kn_ref, ok_ref, kb_scr), (cv_ref, vn_ref, ov_ref, vb_scr)):
        new_t = jnp.concatenate([n_ref[0], jnp.zeros((LANES - t_new, ATT_W), F32)], axis=0).T
        for rb in range(ATT_W // LANES):
            rows = slice(rb * LANES, (rb + 1) * LANES)
            old = c_ref[0, 0, rows, :]
            shifted = pltpu.roll(old, win - t_new, 1)
            out_ref[0, 0, rows, 0:win - LANES] = shifted[:, 0:win - LANES]
            out_ref[0, 0, rows, win - LANES:win] = jnp.where(
                tail, pltpu.roll(new_t[rows], LANES - t_new, 1), shifted[:, win - LANES:win])
            scr[rows, 0:win] = old.astype(BF16)
            scr[rows, win:win + LANES] = new_t[rows].astype(BF16)
    q = q_ref[0].astype(F32)
    rows = ATT_HEADS * t_new
    own = (_iota((rows, ATT_W), 0) // t_new) == (_iota((rows, ATT_W), 1) // HEAD_DIM)
    q_stack = jnp.where(own, jnp.concatenate([q] * ATT_HEADS, axis=0), 0.0).astype(BF16)
    s = _dot(q_stack, kb_scr[...]) + bias_ref[...]
    m = jnp.max(s, axis=1, keepdims=True)
    p = jnp.exp2(s - m)
    l = jnp.sum(p, axis=1, keepdims=True)
    o_all = jnp.where(own, _dot_nt(p.astype(BF16), vb_scr[...]) * (1.0 / l), 0.0)
    o = o_all[0:t_new]
    for h in range(1, ATT_HEADS):
        o = o + o_all[h * t_new:(h + 1) * t_new]
    o_ref[0] = o.astype(o_ref.dtype)


def _sample_attn_call(q, k_new, v_new, cache_kt, cache_vt, prev, layer):
    depth, bsz, _, win = cache_kt.shape
    t_new = q.shape[1]
    rows = ATT_HEADS * t_new
    tq = np.arange(rows)[:, None] % t_new
    n = np.arange(win + LANES)[None, :]
    bias = np.where(n < win + t_new, _log_multiplicity(win + tq - n), NEG).astype(np.float32)
    new = pl.BlockSpec((1, t_new, ATT_W), lambda b: (b, 0, 0))
    cache = pl.BlockSpec((1, 1, ATT_W, win), lambda b: (layer, b, 0, 0))
    in_specs = [new, new, new, cache, cache, _const_spec((rows, win + LANES))]
    args = [q, k_new, v_new, cache_kt, cache_vt, jnp.asarray(bias)]
    aliases = {}
    if prev is not None:
        in_specs += [pl.BlockSpec(memory_space=pl.ANY)] * 2
        args += list(prev)
        aliases = {6: 1, 7: 2}
    sd = jax.ShapeDtypeStruct
    return pl.pallas_call(
        functools.partial(_sample_attn_kernel, prev is not None), grid=(bsz,),
        in_specs=in_specs, out_specs=[new, cache, cache],
        out_shape=[sd((bsz, t_new, ATT_W), BF16), sd(cache_kt.shape, F32), sd(cache_vt.shape, F32)],
        scratch_shapes=[pltpu.VMEM((ATT_W, win + LANES), BF16)] * 2,
        input_output_aliases=aliases, name="sample_attn", compiler_params=_params("parallel"),
    )(*args)


FF_CHUNK = 1024


def _mlp_kernel(h_ref, oh_ref, oa_ref, og_ref, p_ref, an_ref, wo_ref, npm_ref, npre_ref, wup_ref, wdn_ref,
                npost_ref, wpg_ref, wple_ref, out_ref):
    half = h_ref.shape[0] // 2
    halves = [slice(0, half), slice(half, 2 * half)]
    oa = [(_rms(oa_ref[s, :].astype(F32)) * an_ref[...]).astype(BF16) for s in halves]
    mix = [_dot(jnp.concatenate([oh_ref[s, :], oa[i], og_ref[s, :]], axis=1), wo_ref[...])
           for i, s in enumerate(halves)]
    ple = [_dot(p_ref[s, :].astype(BF16), wple_ref[...]) for s in halves]
    h = [h_ref[s, :] + _rms(mix[i]) * npm_ref[...] for i, s in enumerate(halves)]
    xn = [(_rms(x) * npre_ref[...]).astype(BF16) for x in h]
    acc = [None, None]
    for c in range(D_FF // FF_CHUNK):
        cols = slice(c * FF_CHUNK, (c + 1) * FF_CHUNK)
        u = [jnp.maximum(_dot(x, wup_ref[:, cols]), 0.0) for x in xn]
        part = [_dot((x * x).astype(BF16), wdn_ref[cols, :]) for x in u]
        acc = [p if a is None else a + p for a, p in zip(acc, part)]
    h = [x + _rms(a) * npost_ref[...] for x, a in zip(h, acc)]
    gate = [_sigmoid(_dot(x.astype(BF16), wpg_ref[...])) for x in h]
    for i, s in enumerate(halves):
        out_ref[s, :] = h[i] + gate[i] * ple[i]


def _mlp_call(h2d, oh, oa, og, p3d, layer, an, wo, npm, npre, wup, wdn, npost, wpg, wple, tm):
    n = h2d.shape[0]
    row = lambda w: pl.BlockSpec((tm, w), lambda i: (i, 0))
    lspec = lambda *shape: _layer_spec(shape, layer)
    vec = lspec(1, D_MODEL)
    return pl.pallas_call(
        _mlp_kernel, grid=(n // tm,),
        in_specs=[row(D_MODEL), row(HG_W), row(ATT_W), row(GLA_VW),
                  pl.BlockSpec((None, tm, D_PLE), lambda i: (layer, i, 0)), lspec(1, ATT_W),
                  lspec(D_MODEL, D_MODEL), vec, vec, lspec(D_MODEL, D_FF), lspec(D_FF, D_MODEL), vec,
                  lspec(D_MODEL, D_MODEL), lspec(D_PLE, D_MODEL)],
        out_specs=row(D_MODEL), out_shape=jax.ShapeDtypeStruct((n, D_MODEL), F32), name="mlp",
        compiler_params=_params("parallel"),
    )(h2d, oh, oa, og, p3d, an, wo, npm, npre, wup, wdn, npost, wpg, wple)


def _pack_w_in(w_in):
    splits = np.cumsum([0, HG_W, HG_W, HG_W, HG_W, ATT_W, ATT_W, ATT_W, GLA_KW, GLA_KW, GLA_VW,
                        GLA_GATE_RANK, GLA_VW])
    col = lambda i: w_in[..., splits[i]:splits[i + 1]]
    padc = lambda a, w: jnp.pad(a, ((0, 0), (0, 0), (0, w - a.shape[-1])))
    parts = [col(0), col(1), col(2), col(3), col(4), col(5), col(6),
             padc(col(7), GLA_KP), padc(col(8), GLA_KP), col(9), col(11), padc(col(10), LANES)]
    return jnp.concatenate(parts, axis=-1).astype(BF16)


def _rope_tables(pos):
    half = ROT_DIM // 2
    inv = jnp.exp(-math.log(ROPE_THETA) * jnp.arange(half, dtype=F32) * (2.0 / ROT_DIM))
    ang = pos[:, None] * inv[None, :]
    cos, sin = jnp.cos(ang), jnp.sin(ang)
    d = np.arange(LANES) % HEAD_DIM
    first, second = d < half, (d >= half) & (d < ROT_DIM)
    idx = np.where(second, d - half, np.where(first, d, 0))
    cos_t = jnp.where(first | second, cos[:, idx], 1.0)
    sin_a = jnp.where(first, -sin[:, idx], 0.0)
    sin_b = jnp.where(second, sin[:, idx], 0.0)
    return cos_t, sin_a, sin_b


def kernel(x_prompt, x_sample, state_hgrn, state_gla, cache_k, cache_v, p_prompt, p_sample, norm_pre_mix, w_in, hgrn_lb, hgrn_norm, attn_norm, gla_w_gate2, gla_b_gate, gla_norm, w_out, norm_post_mix, norm_pre_mlp, w_up, w_down, norm_post_mlp, w_ple_gate, w_ple):
    depth = w_in.shape[0]
    bp, tp, _ = x_prompt.shape
    bs, ts, _ = x_sample.shape
    win = cache_k.shape[2]

    lb_cum = jnp.cumsum(jax.nn.softmax(hgrn_lb.astype(F32), axis=0), axis=0)
    lower_bounds = lb_cum - lb_cum[0:1]

    tm_p = min(512, tp)
    tm_s = bs * ts
    rope_p = _rope_tables(jnp.arange(tp, dtype=F32))
    rope_s = tuple(jnp.tile(t, (bs, 1)) for t in _rope_tables(jnp.arange(ts, dtype=F32) + PAST_LEN))
    to_t = lambda a: jnp.transpose(a, (0, 1, 3, 4, 2)).reshape(depth, a.shape[1], ATT_W, a.shape[2])
    from_t = lambda a: jnp.transpose(a.reshape(depth, a.shape[1], ATT_HEADS, HEAD_DIM, a.shape[3]), (0, 1, 4, 2, 3))
    ck_t, cv_t = to_t(cache_k), to_t(cache_v)
    pp3 = p_prompt.reshape(depth, bp * tp, D_PLE)
    ps3 = p_sample.reshape(depth, bs * ts, D_PLE)

    hp = x_prompt.reshape(bp * tp, D_MODEL)
    hs = x_sample.reshape(bs * ts, D_MODEL)
    kv_p = None
    kv_s = None
    hg_p, gl_p, hg_s, gl_s = [], [], [], []
    row = lambda a: a.reshape(depth, 1, -1)
    w_all = _pack_w_in(w_in)
    wg2 = jnp.pad(gla_w_gate2, ((0, 0), (0, LANES - GLA_GATE_RANK), (0, GLA_KP - GLA_KW))).astype(BF16)
    bg = row(jnp.pad(gla_b_gate, ((0, 0), (0, GLA_KP - GLA_KW))))
    proj_w = (row(norm_pre_mix), w_all, row(lower_bounds))
    mlp_w = (row(attn_norm), w_out.astype(BF16), row(norm_post_mix), row(norm_pre_mlp), w_up.astype(BF16),
             w_down.astype(BF16), row(norm_post_mlp), w_ple_gate.astype(BF16), w_ple.astype(BF16))
    hgrn_nw, gla_nw = row(hgrn_norm), row(gla_norm)
    for i in range(depth):
        (hq, hk, hv, hlf, hgt, gq, gk, gv, gla, ggt, *att_streams, k_t, v_t) = _proj_call(
            hp, *proj_w, rope_p, wg2, bg, kv_p, i, i, depth, tm_p, tp)
        kv_p = (k_t, v_t)
        sh = lambda a: a.reshape(bp, tp, a.shape[-1])
        oh, s_h = _recur_call(sh(hq), sh(hk), sh(hv), sh(hlf), sh(hgt), hgrn_nw, i, None,
                              HG_HEADS, HG_DK, HG_DV, "hgrn")
        og, s_g = _recur_call(sh(gq), sh(gk), sh(gv), sh(gla), sh(ggt), gla_nw, i, None,
                              GLA_HEADS, GLA_DK, GLA_DV, "gla")
        oa = _prompt_attn_call(att_streams, tp)
        fl = lambda a: a.reshape(bp * tp, a.shape[-1])
        hp = _mlp_call(hp, fl(oh), fl(oa), fl(og), pp3, i, *mlp_w, tm_p)
        hg_p.append(s_h.reshape(bp, HG_HEADS, HG_DK, HG_DV))
        gl_p.append(s_g.reshape(bp, GLA_HEADS, GLA_DK, GLA_DV))

        (hq, hk, hv, hlf, hgt, gq, gk, gv, gla, ggt, aq, k_new, v_new) = _proj_call(
            hs, *proj_w, rope_s, wg2, bg, None, i, 0, 1, tm_s, None)
        pt = lambda a: a.reshape(bs, ts, a.shape[-1])
        oh, s_h = _recur_call(pt(hq), pt(hk), pt(hv), pt(hlf), pt(hgt), hgrn_nw, i,
                              state_hgrn[i].reshape(bs, HG_W, HG_DV), HG_HEADS, HG_DK, HG_DV, "hgrn_s")
        og, s_g = _recur_call(pt(gq), pt(gk), pt(gv), pt(gla), pt(ggt), gla_nw, i,
                              state_gla[i].reshape(bs, GLA_KW, GLA_DV), GLA_HEADS, GLA_DK, GLA_DV, "gla_s")
        s3 = lambda a: a.reshape(bs, ts, ATT_W)
        oa, ck_new, cv_new = _sample_attn_call(s3(aq), s3(k_new), s3(v_new), ck_t, cv_t, kv_s, i)
        kv_s = (ck_new, cv_new)
        ut = lambda a: a.reshape(bs * ts, a.shape[-1])
        hs = _mlp_call(hs, ut(oh), oa.reshape(bs * ts, ATT_W), ut(og), ps3, i, *mlp_w, tm_s)
        hg_s.append(s_h.reshape(bs, HG_HEADS, HG_DK, HG_DV))
        gl_s.append(s_g.reshape(bs, GLA_HEADS, GLA_DK, GLA_DV))

    return (hp.reshape(bp, tp, D_MODEL), hs.reshape(bs, ts, D_MODEL),
            jnp.stack(hg_p), jnp.stack(gl_p), from_t(kv_p[0]), from_t(kv_p[1]),
            jnp.stack(hg_s), jnp.stack(gl_s), from_t(kv_s[0]), from_t(kv_s[1]))
```

```python
import functools
import itertools
import math

import jax
import jax.numpy as jnp
import numpy as np
from jax import lax
from jax.experimental import pallas as pl
from jax.experimental.pallas import tpu as pltpu

F32 = jnp.float32
BF16 = jnp.bfloat16

D_MODEL = 1024
HEAD_DIM = 64
HG_HEADS, HG_DK, HG_DV = 4, 64, 64
ATT_HEADS = 6
GLA_HEADS, GLA_DK, GLA_DV = 6, 32, 64
GLA_GATE_RANK = 16
GLA_TAU = 16.0
D_FF = 4 * D_MODEL
D_PLE = 256
ROPE_THETA = 500000.0
ROT_DIM = HEAD_DIM // 4
DILATED_PATTERNS = ((128, 1), (512, 4), (2048, 16))
MAX_WINDOW = 2048
PAST_LEN = 16384
EPS = 1e-6

LANES = 128
HG_W = HG_HEADS * HG_DK
ATT_W = ATT_HEADS * HEAD_DIM
GLA_KW = GLA_HEADS * GLA_DK
GLA_KP = 256
GLA_VW = GLA_HEADS * GLA_DV

CHUNK = 64
HALF = CHUNK // 2
RECUR_GROUP = 8
RECUR_TT = 256
QBLK = 128
NEG = -1e30
LOG2_E = math.log2(math.e)
EXP2_CLAMP = 115.0
Q_SCALE = HEAD_DIM ** -0.5 * LOG2_E
VMEM_LIMIT = 56 * 1024 * 1024

C_HG = 0
C_AT = C_HG + 4 * HG_W
C_GL = C_AT + 3 * ATT_W
GL_Q, GL_K, GL_V, GL_R, GL_LR = 0, 256, 512, 896, 1280
GL_COLS = 1408
W_COLS = C_GL + GL_COLS


def _rms(x):
    return x * lax.rsqrt(jnp.mean(x * x, axis=-1, keepdims=True) + EPS)


def _sigmoid(x):
    return 1.0 / (1.0 + jnp.exp(-x))


def _dot(a, b):
    return jnp.dot(a, b, preferred_element_type=F32)


def _dot_nt(a, b):
    return lax.dot_general(a, b, (((1,), (1,)), ((), ())), preferred_element_type=F32)


def _dot_tn(a, b):
    return lax.dot_general(a, b, (((0,), (0,)), ((), ())), preferred_element_type=F32)


def _iota(shape, d):
    return lax.broadcasted_iota(jnp.int32, shape, d)


def _const_spec(shape):
    nd = len(shape)
    return pl.BlockSpec(shape, lambda *_: (0,) * nd, pipeline_mode=pl.Buffered(1))


def _layer_spec(shape, layer):
    nd = len(shape)
    return pl.BlockSpec((None,) + tuple(shape), lambda *_: (layer,) + (0,) * nd, pipeline_mode=pl.Buffered(1))


def _params(*semantics):
    return pltpu.CompilerParams(dimension_semantics=semantics, vmem_limit_bytes=VMEM_LIMIT)


def _proj_kernel(streams, x_ref, gpre_ref, w_ref, lb_ref, cos_ref, sa_ref, sb_ref, wg2_ref, bg_ref,
                 hq_o, hk_o, hv_o, hlf_o, hgt_o, gq_o, gk_o, gv_o, gla_o, ggt_o, *att_refs):
    xn = (_rms(x_ref[...]) * gpre_ref[...]).astype(BF16)

    y_at = _dot(xn, w_ref[:, C_AT:C_AT + 3 * ATT_W])
    y_hg = _dot(xn, w_ref[:, C_HG:C_HG + 4 * HG_W])
    y_gl = _dot(xn, w_ref[:, C_GL:C_GL + GL_COLS])

    y = y_at
    cos_t, sin_a, sin_b = cos_ref[...], sa_ref[...], sb_ref[...]

    def rope(v):
        return v * cos_t + pltpu.roll(v, LANES - ROT_DIM // 2, 1) * sin_a + pltpu.roll(v, ROT_DIM // 2, 1) * sin_b

    if streams:
        *stream_os, ak_o, av_o, att_scr = att_refs
    else:
        aq_o, ak_o, av_o = att_refs
    for j in range(ATT_W // LANES):
        sl = slice(j * LANES, (j + 1) * LANES)
        q_rot = rope(y[:, sl]) * Q_SCALE
        k_rot = rope(y[:, ATT_W + j * LANES:ATT_W + (j + 1) * LANES])
        v_grp = y[:, 2 * ATT_W + j * LANES:2 * ATT_W + (j + 1) * LANES]
        if streams:
            ak_o[0, 0, sl, :] = k_rot.T
            av_o[0, 0, sl, :] = v_grp.T
            n_grp = ATT_W // LANES
            att_scr[j], att_scr[n_grp + j], att_scr[2 * n_grp + j] = q_rot, k_rot, v_grp
        else:
            aq_o[:, sl] = q_rot
            ak_o[:, sl] = k_rot
            av_o[:, sl] = v_grp
    if streams:
        for s_o in stream_os:
            d, n = s_o.shape[1], s_o.shape[2]
            for r in range(d):
                rows = pl.ds(r, n, stride=d) if d > 1 else slice(None)
                for grp in range(att_scr.shape[0]):
                    s_o[0, r, :, grp * LANES:(grp + 1) * LANES] = att_scr[grp, rows, :].astype(BF16)

    y = y_hg
    lb = lb_ref[...]
    f = lb + (1.0 - lb) * _sigmoid(y[:, HG_W:2 * HG_W])
    hg = y[:, 3 * HG_W:4 * HG_W]
    hq_o[...] = y[:, 0:HG_W]
    hk_o[...] = 1.0 - f
    hv_o[...] = y[:, 2 * HG_W:3 * HG_W].astype(BF16)
    hlf_o[...] = jnp.log2(f)
    hgt_o[...] = (hg * _sigmoid(hg)).astype(BF16)

    y = y_gl
    gr = y[:, GL_R:GL_R + GLA_VW]
    z = _dot(y[:, GL_LR:GL_LR + LANES].astype(BF16), wg2_ref[...]) + bg_ref[...]
    log_a = (jnp.minimum(z, 0.0) - jnp.log(1.0 + jnp.exp(-jnp.abs(z)))) * (LOG2_E / GLA_TAU)
    gq_o[...] = y[:, GL_Q:GL_Q + GLA_KP] * (GLA_DK ** -0.5)
    gk_o[...] = y[:, GL_K:GL_K + GLA_KP]
    gv_o[...] = y[:, GL_V:GL_V + GLA_VW].astype(BF16)
    gla_o[...] = log_a
    ggt_o[...] = (gr * _sigmoid(gr)).astype(BF16)


def _proj_call(x2d, gpre, w_all, lb, rope_tabs, wg2, bg, kv_prev, w_layer, layer, depth, tm, seq_len):
    n = x2d.shape[0]
    cos_t, sin_a, sin_b = rope_tabs
    tab_blocks = cos_t.shape[0] // tm
    row = lambda w: pl.BlockSpec((tm, w), lambda i: (i, 0))
    tab = pl.BlockSpec((tm, LANES), lambda i: (i % tab_blocks, 0))
    sd = jax.ShapeDtypeStruct
    lspec = lambda *shape: _layer_spec(shape, w_layer)
    in_specs = [row(D_MODEL), lspec(1, D_MODEL), lspec(D_MODEL, W_COLS), lspec(1, HG_W), tab, tab, tab,
                lspec(LANES, GLA_KP), lspec(1, GLA_KP)]
    args = [x2d, gpre, w_all, lb, cos_t, sin_a, sin_b, wg2, bg]
    out_shape = [sd((n, HG_W), F32), sd((n, HG_W), F32), sd((n, HG_W), BF16), sd((n, HG_W), F32),
                 sd((n, HG_W), BF16), sd((n, GLA_KP), F32), sd((n, GLA_KP), F32),
                 sd((n, GLA_VW), BF16), sd((n, GLA_KP), F32), sd((n, GLA_VW), BF16)]
    out_specs = [row(HG_W)] * 5 + [row(GLA_KP), row(GLA_KP), row(GLA_VW), row(GLA_KP), row(GLA_VW)]
    aliases, scratch = {}, []
    if seq_len is not None:
        nt, bsz = seq_len // tm, n // seq_len
        for _, d in DILATED_PATTERNS:
            out_shape.append(sd((bsz, d, seq_len // d, 3 * ATT_W), BF16))
            out_specs.append(pl.BlockSpec((1, d, tm // d, 3 * ATT_W), lambda i: (i // nt, 0, i % nt, 0)))
        out_shape += [sd((depth, bsz, ATT_W, seq_len), F32)] * 2
        out_specs += [pl.BlockSpec((1, 1, ATT_W, tm), lambda i: (layer, i // nt, 0, i % nt))] * 2
        scratch = [pltpu.VMEM((3 * ATT_W // LANES, tm, LANES), F32)]
        if kv_prev is not None:
            in_specs += [pl.BlockSpec(memory_space=pl.ANY)] * 2
            args += list(kv_prev)
            aliases = {len(args) - 2: len(out_shape) - 2, len(args) - 1: len(out_shape) - 1}
    else:
        out_shape += [sd((n, ATT_W), F32)] * 3
        out_specs += [row(ATT_W)] * 3
    n_in = len(args)

    def body(*refs):
        _proj_kernel(seq_len is not None, *refs[:9], *refs[n_in:])

    return pl.pallas_call(
        body, grid=(n // tm,), in_specs=in_specs, out_specs=out_specs, out_shape=out_shape,
        scratch_shapes=scratch, input_output_aliases=aliases, name="proj", compiler_params=_params("parallel"),
    )(*args)


class _Recurrence:
    def __init__(self, heads, dk, dv, in_refs, out_refs, s_scr):
        self.heads, self.dk, self.dv = heads, dk, dv
        self.q_ref, self.k_ref, self.v_ref, self.g_ref, self.gate_ref, self.nw_ref = in_refs[:6]
        self.s0_ref = in_refs[6] if len(in_refs) > 6 else None
        self.o_ref, self.s_out_ref = out_refs
        self.s_scr = s_scr
        self.group, self.t_tile, self.kwp = self.q_ref.shape
        self.kw, self.vw = heads * dk, heads * dv
        per_k = LANES // dk
        self.per_v = LANES // dv
        kwp, vw = self.kwp, self.vw
        self.tri = (_iota((CHUNK, CHUNK), 0) >= _iota((CHUNK, CHUNK), 1)).astype(BF16)
        self.causal = [(_iota((per_k * HALF, CHUNK), 0) % HALF) + h * HALF >= _iota((per_k * HALF, CHUNK), 1)
                       for h in range(2)]
        self.own_head = ((_iota((per_k * HALF, LANES), 0) // HALF)
                         == (_iota((per_k * HALF, LANES), 1) // dk)).astype(BF16)
        self.bd_mask = (_iota((vw, kwp), 0) // dv) == (_iota((vw, kwp), 1) // dk)
        self.pool = jnp.where((_iota((vw, vw), 0) // dv) == (_iota((vw, vw), 1) // dv), 1.0 / dv, 0.0).astype(BF16)
        self.v_head = _iota((CHUNK, LANES), 1) // dv
        self.k_groups = [(kg, min(per_k, heads - kg * per_k)) for kg in range(kwp // LANES) if heads > kg * per_k]
        self.short = self.t_tile < CHUNK

    def init_state(self):
        for g in range(self.group):
            if self.s0_ref is None:
                self.s_scr[g] = jnp.zeros((self.vw, self.kwp), F32)
                continue
            s_nat = self.s0_ref[g]
            if self.kwp > self.kw:
                s_nat = jnp.concatenate([s_nat, jnp.zeros((self.kwp - self.kw, self.dv), F32)], axis=0)
            tiled = jnp.concatenate([s_nat] * self.heads, axis=1)
            self.s_scr[g] = jnp.where(self.bd_mask, tiled.T, 0.0)

    def write_state(self):
        for g in range(self.group):
            s_bd = self.s_scr[g].T
            s_nat = s_bd[:, 0:self.dv]
            for h in range(1, self.heads):
                s_nat = s_nat + s_bd[:, h * self.dv:(h + 1) * self.dv]
            self.s_out_ref[g] = s_nat[0:self.kw]

    def _load(self, ref, g, rows):
        if not self.short:
            return ref[g, rows, :]
        x = ref[g].astype(F32)
        return jnp.concatenate([x, jnp.zeros((CHUNK - self.t_tile, x.shape[1]), F32)], axis=0)

    @staticmethod
    def _split_dot(a, x, terms):
        acc = None
        for _ in range(terms):
            hi = x.astype(BF16)
            part = _dot(a, hi)
            acc = part if acc is None else acc + part
            x = x - hi.astype(F32)
        return acc

    def chunk_stages(self, rows):
        seqs = range(self.group)
        load, s_scr = self._load, self.s_scr
        b = [self._split_dot(self.tri, load(self.g_ref, g, rows), 2) for g in seqs]
        yield
        q_in, k_in, q_st, k_st, dec = [], [], [], [], []
        for g in seqs:
            q = load(self.q_ref, g, rows).astype(F32)
            k = load(self.k_ref, g, rows).astype(F32)
            b_half, b_last = b[g][HALF - 1:HALF, :], b[g][CHUNK - 1:CHUNK, :]
            refs_g = (0.5 * b_half, 0.5 * (b_half + b_last))
            q_in.append([(q[h * HALF:(h + 1) * HALF] * jnp.exp2(jnp.minimum(
                b[g][h * HALF:(h + 1) * HALF] - refs_g[h], EXP2_CLAMP))).astype(BF16) for h in range(2)])
            k_in.append([(k * jnp.exp2(jnp.minimum(r - b[g], EXP2_CLAMP))).astype(BF16) for r in refs_g])
            q_st.append((q * jnp.exp2(b[g])).astype(BF16))
            k_st.append((k * jnp.exp2(b_last - b[g])).astype(BF16))
            dec.append(jnp.exp2(b_last))
        yield
        a = []
        for g in seqs:
            per_group = []
            for kg, nh in self.k_groups:
                sl = slice(kg * LANES, (kg + 1) * LANES)
                halves = []
                for h in range(2):
                    stack = jnp.concatenate([q_in[g][h][:, sl]] * nh, axis=0) * self.own_head[0:nh * HALF]
                    halves.append(_dot_nt(stack, k_in[g][h][:, sl]))
                per_group.append(halves)
            a.append(per_group)
        yield
        s_old = [s_scr[g] for g in seqs]
        o_inter = [_dot_nt(q_st[g], s_old[g].astype(BF16)) for g in seqs]
        v = [load(self.v_ref, g, rows).astype(BF16) for g in seqs]
        u = [_dot_tn(v[g], k_st[g]) for g in seqs]
        yield
        for g in seqs:
            s_scr[g] = s_old[g] * dec[g] + jnp.where(self.bd_mask, u[g], 0.0)
        o = []
        for g in seqs:
            a_heads = []
            for (kg, nh), ag in zip(self.k_groups, a[g]):
                am = [jnp.where(self.causal[h][0:nh * HALF], ag[h], 0.0).astype(BF16) for h in range(2)]
                a_heads += [jnp.concatenate([am[h][e * HALF:(e + 1) * HALF] for h in range(2)], axis=0)
                            for e in range(nh)]
            pieces = []
            for j in range(self.vw // LANES):
                v_grp = v[g][:, j * LANES:(j + 1) * LANES]
                acc = None
                for e in range(self.per_v):
                    oe = _dot(a_heads[j * self.per_v + e], v_grp)
                    acc = oe if acc is None else jnp.where(self.v_head == e, oe, acc)
                pieces.append(acc)
            o.append(o_inter[g] + jnp.concatenate(pieces, axis=1))
        yield
        ms_all = _dot(jnp.concatenate([(o[g] * o[g]).astype(BF16) for g in seqs], axis=0), self.pool)
        yield
        nw = self.nw_ref[...]
        for g in seqs:
            gate = load(self.gate_ref, g, rows).astype(F32)
            out = (o[g] * lax.rsqrt(ms_all[g * CHUNK:(g + 1) * CHUNK] + EPS) * nw * gate).astype(self.o_ref.dtype)
            if self.short:
                self.o_ref[g] = out[0:self.t_tile]
            else:
                self.o_ref[g, rows, :] = out


def _recur_kernel(shapes, has_init, *refs):
    n_in = 7 if has_init else 6
    n = len(shapes)
    probs = [_Recurrence(*shp, refs[i * n_in:(i + 1) * n_in], refs[n * n_in + 2 * i:n * n_in + 2 * i + 2],
                         refs[n * (n_in + 2) + i]) for i, shp in enumerate(shapes)]
    t_idx = pl.program_id(1)

    @pl.when(t_idx == 0)
    def _():
        for p in probs:
            p.init_state()

    def step(c, carry):
        rows = pl.ds(pl.multiple_of(c * CHUNK, CHUNK), CHUNK)
        for _ in itertools.zip_longest(*[p.chunk_stages(rows) for p in probs]):
            pass
        return carry

    lax.fori_loop(0, max(1, probs[0].t_tile // CHUNK), step, 0)

    @pl.when(t_idx == pl.num_programs(1) - 1)
    def _():
        for p in probs:
            p.write_state()


def _recur_call(problems, layer, name):
    bsz, t_len, _ = problems[0][0].shape
    group = math.gcd(RECUR_GROUP, bsz)
    tt = min(RECUR_TT, t_len)
    has_init = problems[0][6] is not None
    seq = lambda w: pl.BlockSpec((group, tt, w), lambda b, t: (b, t, 0))
    in_specs, args, out_specs, out_shape, scratch, shapes = [], [], [], [], [], []
    for q, k, v, g, gate, nw, s0, heads, dk, dv in problems:
        kwp, kw, vw = q.shape[2], heads * dk, heads * dv
        st = pl.BlockSpec((group, kw, dv), lambda b, t: (b, 0, 0))
        in_specs += [seq(kwp), seq(kwp), seq(vw), seq(kwp), seq(vw), _layer_spec((1, vw), layer)]
        args += [q, k, v, g, gate, nw]
        if has_init:
            in_specs.append(st)
            args.append(s0)
        out_specs += [seq(vw), st]
        out_shape += [jax.ShapeDtypeStruct((bsz, t_len, vw), BF16), jax.ShapeDtypeStruct((bsz, kw, dv), F32)]
        scratch.append(pltpu.VMEM((group, vw, kwp), F32))
        shapes.append((heads, dk, dv))
    outs = pl.pallas_call(
        functools.partial(_recur_kernel, tuple(shapes), has_init),
        grid=(bsz // group, t_len // tt), in_specs=in_specs, out_specs=out_specs, out_shape=out_shape,
        scratch_shapes=scratch, name=name, compiler_params=_params("parallel", "arbitrary"),
    )(*args)
    return [tuple(outs[2 * i:2 * i + 2]) for i in range(len(problems))]


def _log_multiplicity(delta):
    delta = np.asarray(delta, np.int64)
    cnt = np.zeros(delta.shape, np.float64)
    for w, d in DILATED_PATTERNS:
        cnt += (delta >= 0) & (delta <= w) & (delta % d == 0)
    return np.where(cnt > 0, np.log2(np.maximum(cnt, 1.0)), NEG).astype(np.float32)


ATTN_UNITS_PER_GROUP = 16


def _prompt_attn_kernel(*refs):
    n_pat = len(DILATED_PATTERNS)
    qs, ks, vs = refs[0:3 * n_pat:3], refs[1:3 * n_pat:3], refs[2:3 * n_pat:3]
    mask_ref, o_ref = refs[3 * n_pat:3 * n_pat + 2]
    scr = refs[3 * n_pat + 2:]
    nd, md, ld = scr[0::3], scr[1::3], scr[2::3]
    t_len = o_ref.shape[1]
    dils = [d for _, d in DILATED_PATTERNS]
    lo = _iota((QBLK, LANES), 1) < HEAD_DIM

    units = []
    for di, d in enumerate(dils):
        for r in range(d):
            for pb in range(t_len // d // QBLK):
                q0 = pb * QBLK
                k0, klen = (q0, QBLK) if pb == 0 else (q0 - QBLK, 2 * QBLK)
                units.append((di, r, q0, k0, klen))

    eye = (_iota((QBLK, QBLK), 0) == _iota((QBLK, QBLK), 1)).astype(BF16)
    mask_t = mask_ref[...]

    def run(group):
        chains = [(u, e) for u in group for e in (0, 1)]
        s = []
        for (di, r, q0, k0, klen), e in chains:
            q = qs[di][0, r, q0:q0 + QBLK, :]
            q = jnp.where(lo, q, jnp.zeros_like(q)) if e == 0 else jnp.where(lo, jnp.zeros_like(q), q)
            k_aug = jnp.concatenate([ks[di][0, r, k0:k0 + klen, :], mask_t[2 * QBLK - klen:2 * QBLK]], axis=1)
            s.append(_dot_nt(jnp.concatenate([q, eye], axis=1), k_aug))
        for i, (di, r, q0, k0, klen) in enumerate(group):
            m = [jnp.max(x, axis=1, keepdims=True) for x in s[2 * i:2 * i + 2]]
            p = [jnp.exp2(x - mx) for x, mx in zip(s[2 * i:2 * i + 2], m)]
            l = [jnp.sum(x, axis=1, keepdims=True) for x in p]
            num = [_dot(x.astype(BF16), vs[di][0, r, k0:k0 + klen, :]) for x in p]
            n_u = jnp.where(lo, num[0], num[1])
            m_u = jnp.where(lo, m[0], m[1])
            l_u = jnp.where(lo, l[0], l[1])
            if di < n_pat - 1:
                blk = (r, slice(q0, q0 + QBLK))
                nd[di][blk], md[di][blk], ld[di][blk] = n_u, m_u, l_u
                continue
            rows = [(r % d, pl.ds(r // d, QBLK, stride=d_max // d)) for d in dils[:-1]]
            ms = [md[dj][rw] for dj, rw in enumerate(rows)] + [m_u]
            top = functools.reduce(jnp.maximum, ms)
            w = [jnp.exp2(x - top) for x in ms]
            acc = sum(wx * nx for wx, nx in zip(w, [nd[dj][rw] for dj, rw in enumerate(rows)] + [n_u]))
            den = sum(wx * lx for wx, lx in zip(w, [ld[dj][rw] for dj, rw in enumerate(rows)] + [l_u]))
            o_ref[0, pl.ds(r, QBLK, stride=d_max), :] = acc * (1.0 / den)

    d_max = dils[-1]
    for i in range(0, len(units), ATTN_UNITS_PER_GROUP):
        run(units[i:i + ATTN_UNITS_PER_GROUP])


def _prompt_attn_call(streams, t_len):
    bsz = streams[0].shape[0]
    dils = [d for _, d in DILATED_PATTERNS]
    assert all(w // d == QBLK for w, d in DILATED_PATTERNS) and t_len == QBLK * dils[-1] == QBLK * max(dils)
    a = np.arange(QBLK)[:, None]
    c = np.arange(2 * QBLK)[None, :]
    mask_t = jnp.asarray(np.where((c >= a) & (c <= a + QBLK), 0.0, NEG).astype(np.float32).T, dtype=BF16)
    n_grp = ATT_W // LANES
    in_specs, args = [], []
    for s_d in streams:
        d, n = s_d.shape[1], s_d.shape[2]
        for part in range(3):
            in_specs.append(pl.BlockSpec((1, d, n, LANES), lambda b, j, part=part: (b, 0, 0, part * n_grp + j)))
            args.append(s_d)
    scratch = []
    for d in dils[:-1]:
        scratch += [pltpu.VMEM((d, t_len // d, LANES), F32)] * 3
    return pl.pallas_call(
        _prompt_attn_kernel, grid=(bsz, n_grp),
        in_specs=in_specs + [_const_spec((2 * QBLK, QBLK))],
        out_specs=pl.BlockSpec((1, t_len, LANES), lambda b, j: (b, 0, j)),
        out_shape=jax.ShapeDtypeStruct((bsz, t_len, ATT_W), F32), name="prompt_attn",
        scratch_shapes=scratch,
        compiler_params=_params("parallel", "parallel"),
    )(*args, mask_t)


def _sample_attn_kernel(with_prev, *refs):
    if with_prev:
        q_ref, kn_ref, vn_ref, ck_ref, cv_ref, bias_ref, _, _, o_ref, ok_ref, ov_ref, kb_scr, vb_scr = refs
    else:
        q_ref, kn_ref, vn_ref, ck_ref, cv_ref, bias_ref, o_ref, ok_ref, ov_ref, kb_scr, vb_scr = refs
    win = ck_ref.shape[3]
    t_new = kn_ref.shape[1]
    tail = _iota((LANES, LANES), 1) >= LANES - t_new
    for c_ref, n_ref, out_ref, scr in ((ck_ref, kn_ref, ok_ref, kb_scr), (cv_ref, vn_ref, ov_ref, vb_scr)):
        new_t = jnp.concatenate([n_ref[0], jnp.zeros((LANES - t_new, ATT_W), F32)], axis=0).T
        for rb in range(ATT_W // LANES):
            rows = slice(rb * LANES, (rb + 1) * LANES)
            old = c_ref[0, 0, rows, :]
            shifted = pltpu.roll(old, win - t_new, 1)
            out_ref[0, 0, rows, 0:win - LANES] = shifted[:, 0:win - LANES]
            out_ref[0, 0, rows, win - LANES:win] = jnp.where(
                tail, pltpu.roll(new_t[rows], LANES - t_new, 1), shifted[:, win - LANES:win])
            scr[rows, 0:win] = old.astype(BF16)
            scr[rows, win:win + LANES] = new_t[rows].astype(BF16)
    q = q_ref[0].astype(F32)
    rows = ATT_HEADS * t_new
    own = (_iota((rows, ATT_W), 0) // t_new) == (_iota((rows, ATT_W), 1) // HEAD_DIM)
    q_stack = jnp.where(own, jnp.concatenate([q] * ATT_HEADS, axis=0), 0.0).astype(BF16)
    s = _dot(q_stack, kb_scr[...]) + bias_ref[...]
    m = jnp.max(s, axis=1, keepdims=True)
    p = jnp.exp2(s - m)
    l = jnp.sum(p, axis=1, keepdims=True)
    o_all = jnp.where(own, _dot_nt(p.astype(BF16), vb_scr[...]) * (1.0 / l), 0.0)
    o = o_all[0:t_new]
    for h in range(1, ATT_HEADS):
        o = o + o_all[h * t_new:(h + 1) * t_new]
    o_ref[0] = o.astype(o_ref.dtype)


def _sample_attn_call(q, k_new, v_new, cache_kt, cache_vt, prev, layer):
    depth, bsz, _, win = cache_kt.shape
    t_new = q.shape[1]
    rows = ATT_HEADS * t_new
    tq = np.arange(rows)[:, None] % t_new
    n = np.arange(win + LANES)[None, :]
    bias = np.where(n < win + t_new, _log_multiplicity(win + tq - n), NEG).astype(np.float32)
    new = pl.BlockSpec((1, t_new, ATT_W), lambda b: (b, 0, 0))
    cache = pl.BlockSpec((1, 1, ATT_W, win), lambda b: (layer, b, 0, 0))
    in_specs = [new, new, new, cache, cache, _const_spec((rows, win + LANES))]
    args = [q, k_new, v_new, cache_kt, cache_vt, jnp.asarray(bias)]
    aliases = {}
    if prev is not None:
        in_specs += [pl.BlockSpec(memory_space=pl.ANY)] * 2
        args += list(prev)
        aliases = {6: 1, 7: 2}
    sd = jax.ShapeDtypeStruct
    return pl.pallas_call(
        functools.partial(_sample_attn_kernel, prev is not None), grid=(bsz,),
        in_specs=in_specs, out_specs=[new, cache, cache],
        out_shape=[sd((bsz, t_new, ATT_W), BF16), sd(cache_kt.shape, F32), sd(cache_vt.shape, F32)],
        scratch_shapes=[pltpu.VMEM((ATT_W, win + LANES), BF16)] * 2,
        input_output_aliases=aliases, name="sample_attn", compiler_params=_params("parallel"),
    )(*args)


FF_CHUNK = 1024


def _mlp_kernel(h_ref, oh_ref, oa_ref, og_ref, p_ref, an_ref, wo_ref, npm_ref, npre_ref, wup_ref, wdn_ref,
                npost_ref, wpg_ref, wple_ref, out_ref):
    half = h_ref.shape[0] // 2
    halves = [slice(0, half), slice(half, 2 * half)]
    oa = [(_rms(oa_ref[s, :].astype(F32)) * an_ref[...]).astype(BF16) for s in halves]
    mix = [_dot(jnp.concatenate([oh_ref[s, :], oa[i], og_ref[s, :]], axis=1), wo_ref[...])
           for i, s in enumerate(halves)]
    ple = [_dot(p_ref[s, :].astype(BF16), wple_ref[...]) for s in halves]
    h = [h_ref[s, :] + _rms(mix[i]) * npm_ref[...] for i, s in enumerate(halves)]
    xn = [(_rms(x) * npre_ref[...]).astype(BF16) for x in h]
    acc = [None, None]
    for c in range(D_FF // FF_CHUNK):
        cols = slice(c * FF_CHUNK, (c + 1) * FF_CHUNK)
        u = [jnp.maximum(_dot(x, wup_ref[:, cols]), 0.0) for x in xn]
        part = [_dot((x * x).astype(BF16), wdn_ref[cols, :]) for x in u]
        acc = [p if a is None else a + p for a, p in zip(acc, part)]
    h = [x + _rms(a) * npost_ref[...] for x, a in zip(h, acc)]
    gate = [_sigmoid(_dot(x.astype(BF16), wpg_ref[...])) for x in h]
    for i, s in enumerate(halves):
        out_ref[s, :] = h[i] + gate[i] * ple[i]


def _mlp_call(h2d, oh, oa, og, p3d, layer, an, wo, npm, npre, wup, wdn, npost, wpg, wple, tm):
    n = h2d.shape[0]
    row = lambda w: pl.BlockSpec((tm, w), lambda i: (i, 0))
    lspec = lambda *shape: _layer_spec(shape, layer)
    vec = lspec(1, D_MODEL)
    return pl.pallas_call(
        _mlp_kernel, grid=(n // tm,),
        in_specs=[row(D_MODEL), row(HG_W), row(ATT_W), row(GLA_VW),
                  pl.BlockSpec((None, tm, D_PLE), lambda i: (layer, i, 0)), lspec(1, ATT_W),
                  lspec(D_MODEL, D_MODEL), vec, vec, lspec(D_MODEL, D_FF), lspec(D_FF, D_MODEL), vec,
                  lspec(D_MODEL, D_MODEL), lspec(D_PLE, D_MODEL)],
        out_specs=row(D_MODEL), out_shape=jax.ShapeDtypeStruct((n, D_MODEL), F32), name="mlp",
        compiler_params=_params("parallel"),
    )(h2d, oh, oa, og, p3d, an, wo, npm, npre, wup, wdn, npost, wpg, wple)


def _pack_w_in(w_in):
    splits = np.cumsum([0, HG_W, HG_W, HG_W, HG_W, ATT_W, ATT_W, ATT_W, GLA_KW, GLA_KW, GLA_VW,
                        GLA_GATE_RANK, GLA_VW])
    col = lambda i: w_in[..., splits[i]:splits[i + 1]]
    padc = lambda a, w: jnp.pad(a, ((0, 0), (0, 0), (0, w - a.shape[-1])))
    parts = [col(0), col(1), col(2), col(3), col(4), col(5), col(6),
             padc(col(7), GLA_KP), padc(col(8), GLA_KP), col(9), col(11), padc(col(10), LANES)]
    return jnp.concatenate(parts, axis=-1).astype(BF16)


def _rope_tables(pos):
    half = ROT_DIM // 2
    inv = jnp.exp(-math.log(ROPE_THETA) * jnp.arange(half, dtype=F32) * (2.0 / ROT_DIM))
    ang = pos[:, None] * inv[None, :]
    cos, sin = jnp.cos(ang), jnp.sin(ang)
    d = np.arange(LANES) % HEAD_DIM
    first, second = d < half, (d >= half) & (d < ROT_DIM)
    idx = np.where(second, d - half, np.where(first, d, 0))
    cos_t = jnp.where(first | second, cos[:, idx], 1.0)
    sin_a = jnp.where(first, -sin[:, idx], 0.0)
    sin_b = jnp.where(second, sin[:, idx], 0.0)
    return cos_t, sin_a, sin_b


def kernel(x_prompt, x_sample, state_hgrn, state_gla, cache_k, cache_v, p_prompt, p_sample, norm_pre_mix, w_in, hgrn_lb, hgrn_norm, attn_norm, gla_w_gate2, gla_b_gate, gla_norm, w_out, norm_post_mix, norm_pre_mlp, w_up, w_down, norm_post_mlp, w_ple_gate, w_ple):
    depth = w_in.shape[0]
    bp, tp, _ = x_prompt.shape
    bs, ts, _ = x_sample.shape
    win = cache_k.shape[2]

    lb_cum = jnp.cumsum(jax.nn.softmax(hgrn_lb.astype(F32), axis=0), axis=0)
    lower_bounds = lb_cum - lb_cum[0:1]

    tm_p = min(512, tp)
    tm_s = bs * ts
    rope_p = _rope_tables(jnp.arange(tp, dtype=F32))
    rope_s = tuple(jnp.tile(t, (bs, 1)) for t in _rope_tables(jnp.arange(ts, dtype=F32) + PAST_LEN))
    to_t = lambda a: jnp.transpose(a, (0, 1, 3, 4, 2)).reshape(depth, a.shape[1], ATT_W, a.shape[2])
    from_t = lambda a: jnp.transpose(a.reshape(depth, a.shape[1], ATT_HEADS, HEAD_DIM, a.shape[3]), (0, 1, 4, 2, 3))
    ck_t, cv_t = to_t(cache_k), to_t(cache_v)
    pp3 = p_prompt.reshape(depth, bp * tp, D_PLE)
    ps3 = p_sample.reshape(depth, bs * ts, D_PLE)

    hp = x_prompt.reshape(bp * tp, D_MODEL)
    hs = x_sample.reshape(bs * ts, D_MODEL)
    kv_p = None
    kv_s = None
    hg_p, gl_p, hg_s, gl_s = [], [], [], []
    row = lambda a: a.reshape(depth, 1, -1)
    w_all = _pack_w_in(w_in)
    wg2 = jnp.pad(gla_w_gate2, ((0, 0), (0, LANES - GLA_GATE_RANK), (0, GLA_KP - GLA_KW))).astype(BF16)
    bg = row(jnp.pad(gla_b_gate, ((0, 0), (0, GLA_KP - GLA_KW))))
    proj_w = (row(norm_pre_mix), w_all, row(lower_bounds))
    mlp_w = (row(attn_norm), w_out.astype(BF16), row(norm_post_mix), row(norm_pre_mlp), w_up.astype(BF16),
             w_down.astype(BF16), row(norm_post_mlp), w_ple_gate.astype(BF16), w_ple.astype(BF16))
    hgrn_nw, gla_nw = row(hgrn_norm), row(gla_norm)
    for i in range(depth):
        (hq, hk, hv, hlf, hgt, gq, gk, gv, gla, ggt, *att_streams, k_t, v_t) = _proj_call(
            hp, *proj_w, rope_p, wg2, bg, kv_p, i, i, depth, tm_p, tp)
        kv_p = (k_t, v_t)
        sh = lambda a: a.reshape(bp, tp, a.shape[-1])
        (oh, s_h), (og, s_g) = _recur_call(
            [(sh(hq), sh(hk), sh(hv), sh(hlf), sh(hgt), hgrn_nw, None, HG_HEADS, HG_DK, HG_DV),
             (sh(gq), sh(gk), sh(gv), sh(gla), sh(ggt), gla_nw, None, GLA_HEADS, GLA_DK, GLA_DV)], i, "recur")
        oa = _prompt_attn_call(att_streams, tp)
        fl = lambda a: a.reshape(bp * tp, a.shape[-1])
        hp = _mlp_call(hp, fl(oh), fl(oa), fl(og), pp3, i, *mlp_w, tm_p)
        hg_p.append(s_h.reshape(bp, HG_HEADS, HG_DK, HG_DV))
        gl_p.append(s_g.reshape(bp, GLA_HEADS, GLA_DK, GLA_DV))

        (hq, hk, hv, hlf, hgt, gq, gk, gv, gla, ggt, aq, k_new, v_new) = _proj_call(
            hs, *proj_w, rope_s, wg2, bg, None, i, 0, 1, tm_s, None)
        pt = lambda a: a.reshape(bs, ts, a.shape[-1])
        (oh, s_h), (og, s_g) = _recur_call(
            [(pt(hq), pt(hk), pt(hv), pt(hlf), pt(hgt), hgrn_nw, state_hgrn[i].reshape(bs, HG_W, HG_DV),
              HG_HEADS, HG_DK, HG_DV),
             (pt(gq), pt(gk), pt(gv), pt(gla), pt(ggt), gla_nw, state_gla[i].reshape(bs, GLA_KW, GLA_DV),
              GLA_HEADS, GLA_DK, GLA_DV)], i, "recur_s")
        s3 = lambda a: a.reshape(bs, ts, ATT_W)
        oa, ck_new, cv_new = _sample_attn_call(s3(aq), s3(k_new), s3(v_new), ck_t, cv_t, kv_s, i)
        kv_s = (ck_new, cv_new)
        ut = lambda a: a.reshape(bs * ts, a.shape[-1])
        hs = _mlp_call(hs, ut(oh), oa.reshape(bs * ts, ATT_W), ut(og), ps3, i, *mlp_w, tm_s)
        hg_s.append(s_h.reshape(bs, HG_HEADS, HG_DK, HG_DV))
        gl_s.append(s_g.reshape(bs, GLA_HEADS, GLA_DK, GLA_DV))

    return (hp.reshape(bp, tp, D_MODEL), hs.reshape(bs, ts, D_MODEL),
            jnp.stack(hg_p), jnp.stack(gl_p), from_t(kv_p[0]), from_t(kv_p[1]),
            jnp.stack(hg_s), jnp.stack(gl_s), from_t(kv_s[0]), from_t(kv_s[1]))
```

```python
import functools
import itertools
import math

import jax
import jax.numpy as jnp
import numpy as np
from jax import lax
from jax.experimental import pallas as pl
from jax.experimental.pallas import tpu as pltpu

F32 = jnp.float32
BF16 = jnp.bfloat16

D_MODEL = 1024
HEAD_DIM = 64
HG_HEADS, HG_DK, HG_DV = 4, 64, 64
ATT_HEADS = 6
GLA_HEADS, GLA_DK, GLA_DV = 6, 32, 64
GLA_GATE_RANK = 16
GLA_TAU = 16.0
D_FF = 4 * D_MODEL
D_PLE = 256
ROPE_THETA = 500000.0
ROT_DIM = HEAD_DIM // 4
DILATED_PATTERNS = ((128, 1), (512, 4), (2048, 16))
PAST_LEN = 16384
EPS = 1e-6

LANES = 128
HG_W = HG_HEADS * HG_DK
ATT_W = ATT_HEADS * HEAD_DIM
GLA_KW = GLA_HEADS * GLA_DK
GLA_KP = 256
GLA_VW = GLA_HEADS * GLA_DV

CHUNK = 64
HALF = CHUNK // 2
TOKEN_TILE = 512
RECUR_GROUP = 8
RECUR_TT = 256
QBLK = 128
NEG = -1e30
LOG2_E = math.log2(math.e)
EXP2_CLAMP = 115.0
Q_SCALE = HEAD_DIM ** -0.5 * LOG2_E
VMEM_LIMIT = 56 * 1024 * 1024

C_HG = 0
C_AT = C_HG + 4 * HG_W
C_GL = C_AT + 3 * ATT_W
GL_Q, GL_K, GL_V, GL_R, GL_LR = 0, 256, 512, 896, 1280
GL_COLS = 1408
W_COLS = C_GL + GL_COLS


def _rms(x):
    return x * lax.rsqrt(jnp.mean(x * x, axis=-1, keepdims=True) + EPS)


def _sigmoid(x):
    return 1.0 / (1.0 + jnp.exp(-x))


def _dot(a, b):
    return jnp.dot(a, b, preferred_element_type=F32)


def _dot_nt(a, b):
    return lax.dot_general(a, b, (((1,), (1,)), ((), ())), preferred_element_type=F32)


def _dot_tn(a, b):
    return lax.dot_general(a, b, (((0,), (0,)), ((), ())), preferred_element_type=F32)


def _iota(shape, d):
    return lax.broadcasted_iota(jnp.int32, shape, d)


def _const_spec(shape):
    nd = len(shape)
    return pl.BlockSpec(shape, lambda *_: (0,) * nd, pipeline_mode=pl.Buffered(1))


def _layer_spec(shape, layer):
    nd = len(shape)
    return pl.BlockSpec((None,) + tuple(shape), lambda *_: (layer,) + (0,) * nd, pipeline_mode=pl.Buffered(1))


def _params(*semantics):
    return pltpu.CompilerParams(dimension_semantics=semantics, vmem_limit_bytes=VMEM_LIMIT)


def _proj_kernel(streams, x_ref, gpre_ref, w_ref, lb_ref, cos_ref, sa_ref, sb_ref, wg2_ref, bg_ref,
                 hq_o, hk_o, hv_o, hlf_o, hgt_o, gq_o, gk_o, gv_o, gla_o, ggt_o, *att_refs):
    xn = (_rms(x_ref[...]) * gpre_ref[...]).astype(BF16)

    y_at = _dot(xn, w_ref[:, C_AT:C_AT + 3 * ATT_W])
    y_hg = _dot(xn, w_ref[:, C_HG:C_HG + 4 * HG_W])
    y_gl = _dot(xn, w_ref[:, C_GL:C_GL + GL_COLS])

    y = y_at
    cos_t, sin_a, sin_b = cos_ref[...], sa_ref[...], sb_ref[...]

    def rope(v):
        return v * cos_t + pltpu.roll(v, LANES - ROT_DIM // 2, 1) * sin_a + pltpu.roll(v, ROT_DIM // 2, 1) * sin_b

    if streams:
        *stream_os, ak_o, av_o, att_scr = att_refs
    else:
        aq_o, ak_o, av_o = att_refs
    for j in range(ATT_W // LANES):
        sl = slice(j * LANES, (j + 1) * LANES)
        q_rot = rope(y[:, sl]) * Q_SCALE
        k_rot = rope(y[:, ATT_W + j * LANES:ATT_W + (j + 1) * LANES])
        v_grp = y[:, 2 * ATT_W + j * LANES:2 * ATT_W + (j + 1) * LANES]
        if streams:
            ak_o[0, 0, sl, :] = k_rot.T
            av_o[0, 0, sl, :] = v_grp.T
            n_grp = ATT_W // LANES
            att_scr[j], att_scr[n_grp + j], att_scr[2 * n_grp + j] = q_rot, k_rot, v_grp
        else:
            aq_o[:, sl] = q_rot
            ak_o[:, sl] = k_rot
            av_o[:, sl] = v_grp
    if streams:
        for s_o in stream_os:
            d, n = s_o.shape[1], s_o.shape[2]
            for r in range(d):
                rows = pl.ds(r, n, stride=d) if d > 1 else slice(None)
                for grp in range(att_scr.shape[0]):
                    s_o[0, r, :, grp * LANES:(grp + 1) * LANES] = att_scr[grp, rows, :].astype(BF16)

    y = y_hg
    lb = lb_ref[...]
    f = lb + (1.0 - lb) * _sigmoid(y[:, HG_W:2 * HG_W])
    hg = y[:, 3 * HG_W:4 * HG_W]
    hq_o[...] = y[:, 0:HG_W]
    hk_o[...] = 1.0 - f
    hv_o[...] = y[:, 2 * HG_W:3 * HG_W].astype(BF16)
    hlf_o[...] = jnp.log2(f)
    hgt_o[...] = (hg * _sigmoid(hg)).astype(BF16)

    y = y_gl
    gr = y[:, GL_R:GL_R + GLA_VW]
    z = _dot(y[:, GL_LR:GL_LR + LANES].astype(BF16), wg2_ref[...]) + bg_ref[...]
    log_a = (jnp.minimum(z, 0.0) - jnp.log(1.0 + jnp.exp(-jnp.abs(z)))) * (LOG2_E / GLA_TAU)
    gq_o[...] = y[:, GL_Q:GL_Q + GLA_KP] * (GLA_DK ** -0.5)
    gk_o[...] = y[:, GL_K:GL_K + GLA_KP]
    gv_o[...] = y[:, GL_V:GL_V + GLA_VW].astype(BF16)
    gla_o[...] = log_a
    ggt_o[...] = (gr * _sigmoid(gr)).astype(BF16)


def _proj_call(x2d, gpre, w_all, lb, rope_tabs, wg2, bg, kv_prev, w_layer, layer, depth, tm, seq_len):
    n = x2d.shape[0]
    cos_t, sin_a, sin_b = rope_tabs
    tab_blocks = cos_t.shape[0] // tm
    row = lambda w: pl.BlockSpec((tm, w), lambda i: (i, 0))
    tab = pl.BlockSpec((tm, LANES), lambda i: (i % tab_blocks, 0))
    sd = jax.ShapeDtypeStruct
    lspec = lambda *shape: _layer_spec(shape, w_layer)
    in_specs = [row(D_MODEL), lspec(1, D_MODEL), lspec(D_MODEL, W_COLS), lspec(1, HG_W), tab, tab, tab,
                lspec(LANES, GLA_KP), lspec(1, GLA_KP)]
    args = [x2d, gpre, w_all, lb, cos_t, sin_a, sin_b, wg2, bg]
    out_shape = [sd((n, HG_W), F32), sd((n, HG_W), F32), sd((n, HG_W), BF16), sd((n, HG_W), F32),
                 sd((n, HG_W), BF16), sd((n, GLA_KP), F32), sd((n, GLA_KP), F32),
                 sd((n, GLA_VW), BF16), sd((n, GLA_KP), F32), sd((n, GLA_VW), BF16)]
    out_specs = [row(HG_W)] * 5 + [row(GLA_KP), row(GLA_KP), row(GLA_VW), row(GLA_KP), row(GLA_VW)]
    aliases, scratch = {}, []
    if seq_len is not None:
        nt, bsz = seq_len // tm, n // seq_len
        for _, d in DILATED_PATTERNS:
            out_shape.append(sd((bsz, d, seq_len // d, 3 * ATT_W), BF16))
            out_specs.append(pl.BlockSpec((1, d, tm // d, 3 * ATT_W), lambda i: (i // nt, 0, i % nt, 0)))
        out_shape += [sd((depth, bsz, ATT_W, seq_len), F32)] * 2
        out_specs += [pl.BlockSpec((1, 1, ATT_W, tm), lambda i: (layer, i // nt, 0, i % nt))] * 2
        scratch = [pltpu.VMEM((3 * ATT_W // LANES, tm, LANES), F32)]
        if kv_prev is not None:
            in_specs += [pl.BlockSpec(memory_space=pl.ANY)] * 2
            args += list(kv_prev)
            aliases = {len(args) - 2: len(out_shape) - 2, len(args) - 1: len(out_shape) - 1}
    else:
        out_shape += [sd((n, ATT_W), F32)] * 3
        out_specs += [row(ATT_W)] * 3
    n_in = len(args)

    def body(*refs):
        _proj_kernel(seq_len is not None, *refs[:9], *refs[n_in:])

    return pl.pallas_call(
        body, grid=(n // tm,), in_specs=in_specs, out_specs=out_specs, out_shape=out_shape,
        scratch_shapes=scratch, input_output_aliases=aliases, name="proj", compiler_params=_params("parallel"),
    )(*args)


class _Recurrence:
    def __init__(self, heads, dk, dv, in_refs, out_refs, s_scr):
        self.heads, self.dk, self.dv = heads, dk, dv
        self.q_ref, self.k_ref, self.v_ref, self.g_ref, self.gate_ref, self.nw_ref = in_refs[:6]
        self.s0_ref = in_refs[6] if len(in_refs) > 6 else None
        self.o_ref, self.s_out_ref = out_refs
        self.s_scr = s_scr
        self.group, self.t_tile, self.kwp = self.q_ref.shape
        self.kw, self.vw = heads * dk, heads * dv
        per_k = LANES // dk
        self.per_v = LANES // dv
        kwp, vw = self.kwp, self.vw
        self.tri = (_iota((CHUNK, CHUNK), 0) >= _iota((CHUNK, CHUNK), 1)).astype(BF16)
        self.causal = [(_iota((per_k * HALF, CHUNK), 0) % HALF) + h * HALF >= _iota((per_k * HALF, CHUNK), 1)
                       for h in range(2)]
        self.own_head = ((_iota((per_k * HALF, LANES), 0) // HALF)
                         == (_iota((per_k * HALF, LANES), 1) // dk)).astype(BF16)
        self.bd_mask = (_iota((vw, kwp), 0) // dv) == (_iota((vw, kwp), 1) // dk)
        self.pool = jnp.where((_iota((vw, vw), 0) // dv) == (_iota((vw, vw), 1) // dv), 1.0 / dv, 0.0).astype(BF16)
        self.v_head = _iota((CHUNK, LANES), 1) // dv
        self.k_groups = [(kg, min(per_k, heads - kg * per_k)) for kg in range(kwp // LANES) if heads > kg * per_k]
        self.short = self.t_tile < CHUNK

    def init_state(self):
        for g in range(self.group):
            if self.s0_ref is None:
                self.s_scr[g] = jnp.zeros((self.vw, self.kwp), F32)
                continue
            s_nat = self.s0_ref[g]
            if self.kwp > self.kw:
                s_nat = jnp.concatenate([s_nat, jnp.zeros((self.kwp - self.kw, self.dv), F32)], axis=0)
            tiled = jnp.concatenate([s_nat] * self.heads, axis=1)
            self.s_scr[g] = jnp.where(self.bd_mask, tiled.T, 0.0)

    def write_state(self):
        for g in range(self.group):
            s_bd = self.s_scr[g].T
            s_nat = s_bd[:, 0:self.dv]
            for h in range(1, self.heads):
                s_nat = s_nat + s_bd[:, h * self.dv:(h + 1) * self.dv]
            self.s_out_ref[g] = s_nat[0:self.kw]

    def _load(self, ref, g, rows):
        if not self.short:
            return ref[g, rows, :]
        x = ref[g].astype(F32)
        return jnp.concatenate([x, jnp.zeros((CHUNK - self.t_tile, x.shape[1]), F32)], axis=0)

    @staticmethod
    def _split_dot(a, x, terms):
        acc = None
        for _ in range(terms):
            hi = x.astype(BF16)
            part = _dot(a, hi)
            acc = part if acc is None else acc + part
            x = x - hi.astype(F32)
        return acc

    def chunk_stages(self, rows):
        seqs = range(self.group)
        load, s_scr = self._load, self.s_scr
        b = [self._split_dot(self.tri, load(self.g_ref, g, rows), 2) for g in seqs]
        yield
        q_in, k_in, q_st, k_st, dec = [], [], [], [], []
        for g in seqs:
            q = load(self.q_ref, g, rows).astype(F32)
            k = load(self.k_ref, g, rows).astype(F32)
            b_half, b_last = b[g][HALF - 1:HALF, :], b[g][CHUNK - 1:CHUNK, :]
            refs_g = (0.5 * b_half, 0.5 * (b_half + b_last))
            q_in.append([(q[h * HALF:(h + 1) * HALF] * jnp.exp2(jnp.minimum(
                b[g][h * HALF:(h + 1) * HALF] - refs_g[h], EXP2_CLAMP))).astype(BF16) for h in range(2)])
            k_in.append([(k * jnp.exp2(jnp.minimum(r - b[g], EXP2_CLAMP))).astype(BF16) for r in refs_g])
            q_st.append((q * jnp.exp2(b[g])).astype(BF16))
            k_st.append((k * jnp.exp2(b_last - b[g])).astype(BF16))
            dec.append(jnp.exp2(b_last))
        yield
        a = []
        for g in seqs:
            per_group = []
            for kg, nh in self.k_groups:
                sl = slice(kg * LANES, (kg + 1) * LANES)
                halves = []
                for h in range(2):
                    stack = jnp.concatenate([q_in[g][h][:, sl]] * nh, axis=0) * self.own_head[0:nh * HALF]
                    halves.append(_dot_nt(stack, k_in[g][h][:, sl]))
                per_group.append(halves)
            a.append(per_group)
        yield
        s_old = [s_scr[g] for g in seqs]
        o_inter = [_dot_nt(q_st[g], s_old[g].astype(BF16)) for g in seqs]
        v = [load(self.v_ref, g, rows).astype(BF16) for g in seqs]
        u = [_dot_tn(v[g], k_st[g]) for g in seqs]
        yield
        for g in seqs:
            s_scr[g] = s_old[g] * dec[g] + jnp.where(self.bd_mask, u[g], 0.0)
        o = []
        for g in seqs:
            a_heads = []
            for (kg, nh), ag in zip(self.k_groups, a[g]):
                am = [jnp.where(self.causal[h][0:nh * HALF], ag[h], 0.0).astype(BF16) for h in range(2)]
                a_heads += [jnp.concatenate([am[h][e * HALF:(e + 1) * HALF] for h in range(2)], axis=0)
                            for e in range(nh)]
            pieces = []
            for j in range(self.vw // LANES):
                v_grp = v[g][:, j * LANES:(j + 1) * LANES]
                acc = None
                for e in range(self.per_v):
                    oe = _dot(a_heads[j * self.per_v + e], v_grp)
                    acc = oe if acc is None else jnp.where(self.v_head == e, oe, acc)
                pieces.append(acc)
            o.append(o_inter[g] + jnp.concatenate(pieces, axis=1))
        yield
        ms_all = _dot(jnp.concatenate([(o[g] * o[g]).astype(BF16) for g in seqs], axis=0), self.pool)
        yield
        nw = self.nw_ref[...]
        for g in seqs:
            gate = load(self.gate_ref, g, rows).astype(F32)
            out = (o[g] * lax.rsqrt(ms_all[g * CHUNK:(g + 1) * CHUNK] + EPS) * nw * gate).astype(self.o_ref.dtype)
            if self.short:
                self.o_ref[g] = out[0:self.t_tile]
            else:
                self.o_ref[g, rows, :] = out


def _recur_kernel(shapes, has_init, *refs):
    n_in = 7 if has_init else 6
    n = len(shapes)
    probs = [_Recurrence(*shp, refs[i * n_in:(i + 1) * n_in], refs[n * n_in + 2 * i:n * n_in + 2 * i + 2],
                         refs[n * (n_in + 2) + i]) for i, shp in enumerate(shapes)]
    t_idx = pl.program_id(1)

    @pl.when(t_idx == 0)
    def _():
        for p in probs:
            p.init_state()

    def step(c, carry):
        rows = pl.ds(pl.multiple_of(c * CHUNK, CHUNK), CHUNK)
        for _ in itertools.zip_longest(*[p.chunk_stages(rows) for p in probs]):
            pass
        return carry

    lax.fori_loop(0, max(1, probs[0].t_tile // CHUNK), step, 0)

    @pl.when(t_idx == pl.num_programs(1) - 1)
    def _():
        for p in probs:
            p.write_state()


def _recur_call(problems, layer, name):
    bsz, t_len, _ = problems[0][0].shape
    group = math.gcd(RECUR_GROUP, bsz)
    tt = min(RECUR_TT, t_len)
    has_init = problems[0][6] is not None
    seq = lambda w: pl.BlockSpec((group, tt, w), lambda b, t: (b, t, 0))
    in_specs, args, out_specs, out_shape, scratch, shapes = [], [], [], [], [], []
    for q, k, v, g, gate, nw, s0, heads, dk, dv in problems:
        kwp, kw, vw = q.shape[2], heads * dk, heads * dv
        st = pl.BlockSpec((group, kw, dv), lambda b, t: (b, 0, 0))
        in_specs += [seq(kwp), seq(kwp), seq(vw), seq(kwp), seq(vw), _layer_spec((1, vw), layer)]
        args += [q, k, v, g, gate, nw]
        if has_init:
            in_specs.append(st)
            args.append(s0)
        out_specs += [seq(vw), st]
        out_shape += [jax.ShapeDtypeStruct((bsz, t_len, vw), BF16), jax.ShapeDtypeStruct((bsz, kw, dv), F32)]
        scratch.append(pltpu.VMEM((group, vw, kwp), F32))
        shapes.append((heads, dk, dv))
    outs = pl.pallas_call(
        functools.partial(_recur_kernel, tuple(shapes), has_init),
        grid=(bsz // group, t_len // tt), in_specs=in_specs, out_specs=out_specs, out_shape=out_shape,
        scratch_shapes=scratch, name=name, compiler_params=_params("parallel", "arbitrary"),
    )(*args)
    return [tuple(outs[2 * i:2 * i + 2]) for i in range(len(problems))]


def _log_multiplicity(delta):
    delta = np.asarray(delta, np.int64)
    cnt = np.zeros(delta.shape, np.float64)
    for w, d in DILATED_PATTERNS:
        cnt += (delta >= 0) & (delta <= w) & (delta % d == 0)
    return np.where(cnt > 0, np.log2(np.maximum(cnt, 1.0)), NEG).astype(np.float32)


ATTN_UNITS_PER_GROUP = 16


def _prompt_attn_kernel(*refs):
    n_pat = len(DILATED_PATTERNS)
    qs, ks, vs = refs[0:3 * n_pat:3], refs[1:3 * n_pat:3], refs[2:3 * n_pat:3]
    mask_ref, o_ref = refs[3 * n_pat:3 * n_pat + 2]
    scr = refs[3 * n_pat + 2:]
    nd, md, ld = scr[0::3], scr[1::3], scr[2::3]
    t_len = o_ref.shape[1]
    dils = [d for _, d in DILATED_PATTERNS]
    lo = _iota((QBLK, LANES), 1) < HEAD_DIM

    units = []
    for di, d in enumerate(dils):
        for r in range(d):
            for pb in range(t_len // d // QBLK):
                q0 = pb * QBLK
                k0, klen = (q0, QBLK) if pb == 0 else (q0 - QBLK, 2 * QBLK)
                units.append((di, r, q0, k0, klen))

    eye = (_iota((QBLK, QBLK), 0) == _iota((QBLK, QBLK), 1)).astype(BF16)
    mask_t = mask_ref[...]

    def run(group):
        chains = [(u, e) for u in group for e in (0, 1)]
        scores = []
        for (di, r, q0, k0, klen), e in chains:
            q = qs[di][0, r, q0:q0 + QBLK, :]
            q = jnp.where(lo, q, jnp.zeros_like(q)) if e == 0 else jnp.where(lo, jnp.zeros_like(q), q)
            k_aug = jnp.concatenate([ks[di][0, r, k0:k0 + klen, :], mask_t[2 * QBLK - klen:2 * QBLK]], axis=1)
            scores.append(_dot_nt(jnp.concatenate([q, eye], axis=1), k_aug))
        for i, (di, r, q0, k0, klen) in enumerate(group):
            s = scores[2 * i:2 * i + 2]
            m = [jnp.max(x, axis=1, keepdims=True) for x in s]
            p = [jnp.exp2(x - mx) for x, mx in zip(s, m)]
            l = [jnp.sum(x, axis=1, keepdims=True) for x in p]
            num = [_dot(x.astype(BF16), vs[di][0, r, k0:k0 + klen, :]) for x in p]
            n_u = jnp.where(lo, num[0], num[1])
            m_u = jnp.where(lo, m[0], m[1])
            l_u = jnp.where(lo, l[0], l[1])
            if di < n_pat - 1:
                blk = (r, slice(q0, q0 + QBLK))
                nd[di][blk], md[di][blk], ld[di][blk] = n_u, m_u, l_u
                continue
            rows = [(r % d, pl.ds(r // d, QBLK, stride=d_max // d)) for d in dils[:-1]]
            ms = [md[dj][rw] for dj, rw in enumerate(rows)] + [m_u]
            top = functools.reduce(jnp.maximum, ms)
            w = [jnp.exp2(x - top) for x in ms]
            acc = sum(wx * nx for wx, nx in zip(w, [nd[dj][rw] for dj, rw in enumerate(rows)] + [n_u]))
            den = sum(wx * lx for wx, lx in zip(w, [ld[dj][rw] for dj, rw in enumerate(rows)] + [l_u]))
            o_ref[0, pl.ds(r, QBLK, stride=d_max), :] = acc * (1.0 / den)

    d_max = dils[-1]
    for i in range(0, len(units), ATTN_UNITS_PER_GROUP):
        run(units[i:i + ATTN_UNITS_PER_GROUP])


def _prompt_attn_call(streams, t_len):
    bsz = streams[0].shape[0]
    dils = [d for _, d in DILATED_PATTERNS]
    assert all(w // d == QBLK for w, d in DILATED_PATTERNS) and t_len == QBLK * dils[-1] == QBLK * max(dils)
    a = np.arange(QBLK)[:, None]
    c = np.arange(2 * QBLK)[None, :]
    mask_t = jnp.asarray(np.where((c >= a) & (c <= a + QBLK), 0.0, NEG).astype(np.float32).T, dtype=BF16)
    n_grp = ATT_W // LANES
    in_specs, args = [], []
    for s_d in streams:
        d, n = s_d.shape[1], s_d.shape[2]
        for part in range(3):
            in_specs.append(pl.BlockSpec((1, d, n, LANES), lambda b, j, part=part: (b, 0, 0, part * n_grp + j)))
            args.append(s_d)
    scratch = []
    for d in dils[:-1]:
        scratch += [pltpu.VMEM((d, t_len // d, LANES), F32)] * 3
    return pl.pallas_call(
        _prompt_attn_kernel, grid=(bsz, n_grp),
        in_specs=in_specs + [_const_spec((2 * QBLK, QBLK))],
        out_specs=pl.BlockSpec((1, t_len, LANES), lambda b, j: (b, 0, j)),
        out_shape=jax.ShapeDtypeStruct((bsz, t_len, ATT_W), F32), name="prompt_attn",
        scratch_shapes=scratch,
        compiler_params=_params("parallel", "parallel"),
    )(*args, mask_t)


def _sample_attn_kernel(with_prev, *refs):
    if with_prev:
        q_ref, kn_ref, vn_ref, ck_ref, cv_ref, bias_ref, _, _, o_ref, ok_ref, ov_ref, kb_scr, vb_scr = refs
    else:
        q_ref, kn_ref, vn_ref, ck_ref, cv_ref, bias_ref, o_ref, ok_ref, ov_ref, kb_scr, vb_scr = refs
    win = ck_ref.shape[3]
    t_new = kn_ref.shape[1]
    tail = _iota((LANES, LANES), 1) >= LANES - t_new
    for c_ref, n_ref, out_ref, scr in ((ck_ref, kn_ref, ok_ref, kb_scr), (cv_ref, vn_ref, ov_ref, vb_scr)):
        new_t = jnp.concatenate([n_ref[0], jnp.zeros((LANES - t_new, ATT_W), F32)], axis=0).T
        for rb in range(ATT_W // LANES):
            rows = slice(rb * LANES, (rb + 1) * LANES)
            old = c_ref[0, 0, rows, :]
            shifted = pltpu.roll(old, win - t_new, 1)
            out_ref[0, 0, rows, 0:win - LANES] = shifted[:, 0:win - LANES]
            out_ref[0, 0, rows, win - LANES:win] = jnp.where(
                tail, pltpu.roll(new_t[rows], LANES - t_new, 1), shifted[:, win - LANES:win])
            scr[rows, 0:win] = old.astype(BF16)
            scr[rows, win:win + LANES] = new_t[rows].astype(BF16)
    q = q_ref[0].astype(F32)
    rows = ATT_HEADS * t_new
    own = (_iota((rows, ATT_W), 0) // t_new) == (_iota((rows, ATT_W), 1) // HEAD_DIM)
    q_stack = jnp.where(own, jnp.concatenate([q] * ATT_HEADS, axis=0), 0.0).astype(BF16)
    s = _dot(q_stack, kb_scr[...]) + bias_ref[...]
    m = jnp.max(s, axis=1, keepdims=True)
    p = jnp.exp2(s - m)
    l = jnp.sum(p, axis=1, keepdims=True)
    o_all = jnp.where(own, _dot_nt(p.astype(BF16), vb_scr[...]) * (1.0 / l), 0.0)
    o = o_all[0:t_new]
    for h in range(1, ATT_HEADS):
        o = o + o_all[h * t_new:(h + 1) * t_new]
    o_ref[0] = o.astype(o_ref.dtype)


def _sample_attn_call(q, k_new, v_new, cache_kt, cache_vt, prev, layer):
    depth, bsz, _, win = cache_kt.shape
    t_new = q.shape[1]
    rows = ATT_HEADS * t_new
    tq = np.arange(rows)[:, None] % t_new
    n = np.arange(win + LANES)[None, :]
    bias = np.where(n < win + t_new, _log_multiplicity(win + tq - n), NEG).astype(np.float32)
    new = pl.BlockSpec((1, t_new, ATT_W), lambda b: (b, 0, 0))
    cache = pl.BlockSpec((1, 1, ATT_W, win), lambda b: (layer, b, 0, 0))
    in_specs = [new, new, new, cache, cache, _const_spec((rows, win + LANES))]
    args = [q, k_new, v_new, cache_kt, cache_vt, jnp.asarray(bias)]
    aliases = {}
    if prev is not None:
        in_specs += [pl.BlockSpec(memory_space=pl.ANY)] * 2
        args += list(prev)
        aliases = {6: 1, 7: 2}
    sd = jax.ShapeDtypeStruct
    return pl.pallas_call(
        functools.partial(_sample_attn_kernel, prev is not None), grid=(bsz,),
        in_specs=in_specs, out_specs=[new, cache, cache],
        out_shape=[sd((bsz, t_new, ATT_W), BF16), sd(cache_kt.shape, F32), sd(cache_vt.shape, F32)],
        scratch_shapes=[pltpu.VMEM((ATT_W, win + LANES), BF16)] * 2,
        input_output_aliases=aliases, name="sample_attn", compiler_params=_params("parallel"),
    )(*args)


FF_CHUNK = 1024


def _mlp_kernel(h_ref, oh_ref, oa_ref, og_ref, p_ref, an_ref, wo_ref, npm_ref, npre_ref, wup_ref, wdn_ref,
                npost_ref, wpg_ref, wple_ref, out_ref):
    half = h_ref.shape[0] // 2
    halves = [slice(0, half), slice(half, 2 * half)]
    oa = [(_rms(oa_ref[s, :].astype(F32)) * an_ref[...]).astype(BF16) for s in halves]
    mix = [_dot(jnp.concatenate([oh_ref[s, :], oa[i], og_ref[s, :]], axis=1), wo_ref[...])
           for i, s in enumerate(halves)]
    ple = [_dot(p_ref[s, :].astype(BF16), wple_ref[...]) for s in halves]
    h = [h_ref[s, :] + _rms(mix[i]) * npm_ref[...] for i, s in enumerate(halves)]
    xn = [(_rms(x) * npre_ref[...]).astype(BF16) for x in h]
    acc = [None, None]
    for c in range(D_FF // FF_CHUNK):
        cols = slice(c * FF_CHUNK, (c + 1) * FF_CHUNK)
        u = [jnp.maximum(_dot(x, wup_ref[:, cols]), 0.0) for x in xn]
        part = [_dot((x * x).astype(BF16), wdn_ref[cols, :]) for x in u]
        acc = [p if a is None else a + p for a, p in zip(acc, part)]
    h = [x + _rms(a) * npost_ref[...] for x, a in zip(h, acc)]
    gate = [_sigmoid(_dot(x.astype(BF16), wpg_ref[...])) for x in h]
    for i, s in enumerate(halves):
        out_ref[s, :] = h[i] + gate[i] * ple[i]


def _mlp_call(h2d, oh, oa, og, p3d, layer, an, wo, npm, npre, wup, wdn, npost, wpg, wple, tm):
    n = h2d.shape[0]
    row = lambda w: pl.BlockSpec((tm, w), lambda i: (i, 0))
    lspec = lambda *shape: _layer_spec(shape, layer)
    vec = lspec(1, D_MODEL)
    return pl.pallas_call(
        _mlp_kernel, grid=(n // tm,),
        in_specs=[row(D_MODEL), row(HG_W), row(ATT_W), row(GLA_VW),
                  pl.BlockSpec((None, tm, D_PLE), lambda i: (layer, i, 0)), lspec(1, ATT_W),
                  lspec(D_MODEL, D_MODEL), vec, vec, lspec(D_MODEL, D_FF), lspec(D_FF, D_MODEL), vec,
                  lspec(D_MODEL, D_MODEL), lspec(D_PLE, D_MODEL)],
        out_specs=row(D_MODEL), out_shape=jax.ShapeDtypeStruct((n, D_MODEL), F32), name="mlp",
        compiler_params=_params("parallel"),
    )(h2d, oh, oa, og, p3d, an, wo, npm, npre, wup, wdn, npost, wpg, wple)


def _pack_w_in(w_in):
    splits = np.cumsum([0, HG_W, HG_W, HG_W, HG_W, ATT_W, ATT_W, ATT_W, GLA_KW, GLA_KW, GLA_VW,
                        GLA_GATE_RANK, GLA_VW])
    col = lambda i: w_in[..., splits[i]:splits[i + 1]]
    padc = lambda a, w: jnp.pad(a, ((0, 0), (0, 0), (0, w - a.shape[-1])))
    parts = [col(0), col(1), col(2), col(3), col(4), col(5), col(6),
             padc(col(7), GLA_KP), padc(col(8), GLA_KP), col(9), col(11), padc(col(10), LANES)]
    return jnp.concatenate(parts, axis=-1).astype(BF16)


def _rope_tables(pos):
    half = ROT_DIM // 2
    inv = jnp.exp(-math.log(ROPE_THETA) * jnp.arange(half, dtype=F32) * (2.0 / ROT_DIM))
    ang = pos[:, None] * inv[None, :]
    cos, sin = jnp.cos(ang), jnp.sin(ang)
    d = np.arange(LANES) % HEAD_DIM
    first, second = d < half, (d >= half) & (d < ROT_DIM)
    idx = np.where(second, d - half, np.where(first, d, 0))
    cos_t = jnp.where(first | second, cos[:, idx], 1.0)
    sin_a = jnp.where(first, -sin[:, idx], 0.0)
    sin_b = jnp.where(second, sin[:, idx], 0.0)
    return cos_t, sin_a, sin_b


def kernel(x_prompt, x_sample, state_hgrn, state_gla, cache_k, cache_v, p_prompt, p_sample, norm_pre_mix, w_in, hgrn_lb, hgrn_norm, attn_norm, gla_w_gate2, gla_b_gate, gla_norm, w_out, norm_post_mix, norm_pre_mlp, w_up, w_down, norm_post_mlp, w_ple_gate, w_ple):
    depth = w_in.shape[0]
    bp, tp, _ = x_prompt.shape
    bs, ts, _ = x_sample.shape
    win = cache_k.shape[2]

    lb_cum = jnp.cumsum(jax.nn.softmax(hgrn_lb.astype(F32), axis=0), axis=0)
    lower_bounds = lb_cum - lb_cum[0:1]

    tm_p = min(TOKEN_TILE, tp)
    tm_s = bs * ts
    rope_p = _rope_tables(jnp.arange(tp, dtype=F32))
    rope_s = tuple(jnp.tile(t, (bs, 1)) for t in _rope_tables(jnp.arange(ts, dtype=F32) + PAST_LEN))
    to_t = lambda a: jnp.transpose(a, (0, 1, 3, 4, 2)).reshape(depth, a.shape[1], ATT_W, a.shape[2])
    from_t = lambda a: jnp.transpose(a.reshape(depth, a.shape[1], ATT_HEADS, HEAD_DIM, a.shape[3]), (0, 1, 4, 2, 3))
    ck_t, cv_t = to_t(cache_k), to_t(cache_v)
    pp3 = p_prompt.reshape(depth, bp * tp, D_PLE)
    ps3 = p_sample.reshape(depth, bs * ts, D_PLE)

    hp = x_prompt.reshape(bp * tp, D_MODEL)
    hs = x_sample.reshape(bs * ts, D_MODEL)
    kv_p = None
    kv_s = None
    hg_p, gl_p, hg_s, gl_s = [], [], [], []
    row = lambda a: a.reshape(depth, 1, -1)
    w_all = _pack_w_in(w_in)
    wg2 = jnp.pad(gla_w_gate2, ((0, 0), (0, LANES - GLA_GATE_RANK), (0, GLA_KP - GLA_KW))).astype(BF16)
    bg = row(jnp.pad(gla_b_gate, ((0, 0), (0, GLA_KP - GLA_KW))))
    proj_w = (row(norm_pre_mix), w_all, row(lower_bounds))
    mlp_w = (row(attn_norm), w_out.astype(BF16), row(norm_post_mix), row(norm_pre_mlp), w_up.astype(BF16),
             w_down.astype(BF16), row(norm_post_mlp), w_ple_gate.astype(BF16), w_ple.astype(BF16))
    hgrn_nw, gla_nw = row(hgrn_norm), row(gla_norm)
    for i in range(depth):
        (hq, hk, hv, hlf, hgt, gq, gk, gv, gla, ggt, *att_streams, k_t, v_t) = _proj_call(
            hp, *proj_w, rope_p, wg2, bg, kv_p, i, i, depth, tm_p, tp)
        kv_p = (k_t, v_t)
        sh = lambda a: a.reshape(bp, tp, a.shape[-1])
        (oh, s_h), (og, s_g) = _recur_call(
            [(sh(hq), sh(hk), sh(hv), sh(hlf), sh(hgt), hgrn_nw, None, HG_HEADS, HG_DK, HG_DV),
             (sh(gq), sh(gk), sh(gv), sh(gla), sh(ggt), gla_nw, None, GLA_HEADS, GLA_DK, GLA_DV)], i, "recur")
        oa = _prompt_attn_call(att_streams, tp)
        fl = lambda a: a.reshape(bp * tp, a.shape[-1])
        hp = _mlp_call(hp, fl(oh), fl(oa), fl(og), pp3, i, *mlp_w, tm_p)
        hg_p.append(s_h.reshape(bp, HG_HEADS, HG_DK, HG_DV))
        gl_p.append(s_g.reshape(bp, GLA_HEADS, GLA_DK, GLA_DV))

        (hq, hk, hv, hlf, hgt, gq, gk, gv, gla, ggt, aq, k_new, v_new) = _proj_call(
            hs, *proj_w, rope_s, wg2, bg, None, i, 0, 1, tm_s, None)
        pt = lambda a: a.reshape(bs, ts, a.shape[-1])
        (oh, s_h), (og, s_g) = _recur_call(
            [(pt(hq), pt(hk), pt(hv), pt(hlf), pt(hgt), hgrn_nw, state_hgrn[i].reshape(bs, HG_W, HG_DV),
              HG_HEADS, HG_DK, HG_DV),
             (pt(gq), pt(gk), pt(gv), pt(gla), pt(ggt), gla_nw, state_gla[i].reshape(bs, GLA_KW, GLA_DV),
              GLA_HEADS, GLA_DK, GLA_DV)], i, "recur_s")
        s3 = lambda a: a.reshape(bs, ts, ATT_W)
        oa, ck_new, cv_new = _sample_attn_call(s3(aq), s3(k_new), s3(v_new), ck_t, cv_t, kv_s, i)
        kv_s = (ck_new, cv_new)
        ut = lambda a: a.reshape(bs * ts, a.shape[-1])
        hs = _mlp_call(hs, ut(oh), oa.reshape(bs * ts, ATT_W), ut(og), ps3, i, *mlp_w, tm_s)
        hg_s.append(s_h.reshape(bs, HG_HEADS, HG_DK, HG_DV))
        gl_s.append(s_g.reshape(bs, GLA_HEADS, GLA_DK, GLA_DV))

    return (hp.reshape(bp, tp, D_MODEL), hs.reshape(bs, ts, D_MODEL),
            jnp.stack(hg_p), jnp.stack(gl_p), from_t(kv_p[0]), from_t(kv_p[1]),
            jnp.stack(hg_s), jnp.stack(gl_s), from_t(kv_s[0]), from_t(kv_s[1]))
```

```python
import functools
import itertools
import math

import jax
import jax.numpy as jnp
import numpy as np
from jax import lax
from jax.experimental import pallas as pl
from jax.experimental.pallas import tpu as pltpu

F32 = jnp.float32
BF16 = jnp.bfloat16

D_MODEL = 1024
HEAD_DIM = 64
HG_HEADS, HG_DK, HG_DV = 4, 64, 64
ATT_HEADS = 6
GLA_HEADS, GLA_DK, GLA_DV = 6, 32, 64
GLA_GATE_RANK = 16
GLA_TAU = 16.0
D_FF = 4 * D_MODEL
D_PLE = 256
ROPE_THETA = 500000.0
ROT_DIM = HEAD_DIM // 4
DILATED_PATTERNS = ((128, 1), (512, 4), (2048, 16))
PAST_LEN = 16384
EPS = 1e-6

LANES = 128
HG_W = HG_HEADS * HG_DK
ATT_W = ATT_HEADS * HEAD_DIM
GLA_KW = GLA_HEADS * GLA_DK
GLA_KP = 256
GLA_VW = GLA_HEADS * GLA_DV

CHUNK = 64
HALF = CHUNK // 2
TOKEN_TILE = 512
RECUR_GROUP = 8
RECUR_TT = 256
QBLK = 128
NEG = -1e30
LOG2_E = math.log2(math.e)
EXP2_CLAMP = 115.0
Q_SCALE = HEAD_DIM ** -0.5 * LOG2_E
VMEM_LIMIT = 56 * 1024 * 1024

C_HG = 0
C_AT = C_HG + 4 * HG_W
C_GL = C_AT + 3 * ATT_W
GL_Q, GL_K, GL_V, GL_R, GL_LR = 0, 256, 512, 896, 1280
GL_COLS = 1408
W_COLS = C_GL + GL_COLS


def _rms(x):
    return x * lax.rsqrt(jnp.mean(x * x, axis=-1, keepdims=True) + EPS)


def _sigmoid(x):
    return 1.0 / (1.0 + jnp.exp(-x))


def _dot(a, b):
    return jnp.dot(a, b, preferred_element_type=F32)


def _dot_nt(a, b):
    return lax.dot_general(a, b, (((1,), (1,)), ((), ())), preferred_element_type=F32)


def _dot_tn(a, b):
    return lax.dot_general(a, b, (((0,), (0,)), ((), ())), preferred_element_type=F32)


def _iota(shape, d):
    return lax.broadcasted_iota(jnp.int32, shape, d)


def _const_spec(shape):
    nd = len(shape)
    return pl.BlockSpec(shape, lambda *_: (0,) * nd, pipeline_mode=pl.Buffered(1))


def _layer_spec(shape, layer):
    nd = len(shape)
    return pl.BlockSpec((None,) + tuple(shape), lambda *_: (layer,) + (0,) * nd, pipeline_mode=pl.Buffered(1))


def _params(*semantics):
    return pltpu.CompilerParams(dimension_semantics=semantics, vmem_limit_bytes=VMEM_LIMIT)


def _proj_kernel(streams, x_ref, gpre_ref, w_ref, lb_ref, cos_ref, sa_ref, sb_ref, wg2_ref, bg_ref,
                 hq_o, hk_o, hv_o, hlf_o, hgt_o, gq_o, gk_o, gv_o, gla_o, ggt_o, *att_refs):
    xn = (_rms(x_ref[...]) * gpre_ref[...]).astype(BF16)

    y_at = _dot(xn, w_ref[:, C_AT:C_AT + 3 * ATT_W])
    y_hg = _dot(xn, w_ref[:, C_HG:C_HG + 4 * HG_W])
    y_gl = _dot(xn, w_ref[:, C_GL:C_GL + GL_COLS])

    y = y_at
    cos_t, sin_a, sin_b = cos_ref[...], sa_ref[...], sb_ref[...]

    def rope(v):
        return v * cos_t + pltpu.roll(v, LANES - ROT_DIM // 2, 1) * sin_a + pltpu.roll(v, ROT_DIM // 2, 1) * sin_b

    if streams:
        *stream_os, ak_o, av_o = att_refs
    else:
        aq_o, ak_o, av_o = att_refs
    for j in range(ATT_W // LANES):
        sl = slice(j * LANES, (j + 1) * LANES)
        q_rot = rope(y[:, sl]) * Q_SCALE
        k_rot = rope(y[:, ATT_W + j * LANES:ATT_W + (j + 1) * LANES])
        v_grp = y[:, 2 * ATT_W + j * LANES:2 * ATT_W + (j + 1) * LANES]
        if streams:
            ak_o[0, 0, sl, :] = k_rot.T
            av_o[0, 0, sl, :] = v_grp.T
            for s_o in stream_os:
                d, n = s_o.shape[1], s_o.shape[2]
                for part, x in enumerate((q_rot, k_rot, v_grp)):
                    x = jnp.swapaxes(x.reshape(n, d, LANES), 0, 1) if d > 1 else x.reshape(1, n, LANES)
                    lanes = slice(part * ATT_W + j * LANES, part * ATT_W + (j + 1) * LANES)
                    s_o[0, :, :, lanes] = x.astype(BF16)
        else:
            aq_o[:, sl] = q_rot
            ak_o[:, sl] = k_rot
            av_o[:, sl] = v_grp

    y = y_hg
    lb = lb_ref[...]
    f = lb + (1.0 - lb) * _sigmoid(y[:, HG_W:2 * HG_W])
    hg = y[:, 3 * HG_W:4 * HG_W]
    hq_o[...] = y[:, 0:HG_W]
    hk_o[...] = 1.0 - f
    hv_o[...] = y[:, 2 * HG_W:3 * HG_W].astype(BF16)
    hlf_o[...] = jnp.log2(f)
    hgt_o[...] = (hg * _sigmoid(hg)).astype(BF16)

    y = y_gl
    gr = y[:, GL_R:GL_R + GLA_VW]
    z = _dot(y[:, GL_LR:GL_LR + LANES].astype(BF16), wg2_ref[...]) + bg_ref[...]
    log_a = (jnp.minimum(z, 0.0) - jnp.log(1.0 + jnp.exp(-jnp.abs(z)))) * (LOG2_E / GLA_TAU)
    gq_o[...] = y[:, GL_Q:GL_Q + GLA_KP] * (GLA_DK ** -0.5)
    gk_o[...] = y[:, GL_K:GL_K + GLA_KP]
    gv_o[...] = y[:, GL_V:GL_V + GLA_VW].astype(BF16)
    gla_o[...] = log_a
    ggt_o[...] = (gr * _sigmoid(gr)).astype(BF16)


def _proj_call(x2d, gpre, w_all, lb, rope_tabs, wg2, bg, kv_prev, w_layer, layer, depth, tm, seq_len):
    n = x2d.shape[0]
    cos_t, sin_a, sin_b = rope_tabs
    tab_blocks = cos_t.shape[0] // tm
    row = lambda w: pl.BlockSpec((tm, w), lambda i: (i, 0))
    tab = pl.BlockSpec((tm, LANES), lambda i: (i % tab_blocks, 0))
    sd = jax.ShapeDtypeStruct
    lspec = lambda *shape: _layer_spec(shape, w_layer)
    in_specs = [row(D_MODEL), lspec(1, D_MODEL), lspec(D_MODEL, W_COLS), lspec(1, HG_W), tab, tab, tab,
                lspec(LANES, GLA_KP), lspec(1, GLA_KP)]
    args = [x2d, gpre, w_all, lb, cos_t, sin_a, sin_b, wg2, bg]
    out_shape = [sd((n, HG_W), F32), sd((n, HG_W), F32), sd((n, HG_W), BF16), sd((n, HG_W), F32),
                 sd((n, HG_W), BF16), sd((n, GLA_KP), F32), sd((n, GLA_KP), F32),
                 sd((n, GLA_VW), BF16), sd((n, GLA_KP), F32), sd((n, GLA_VW), BF16)]
    out_specs = [row(HG_W)] * 5 + [row(GLA_KP), row(GLA_KP), row(GLA_VW), row(GLA_KP), row(GLA_VW)]
    aliases, scratch = {}, []
    if seq_len is not None:
        nt, bsz = seq_len // tm, n // seq_len
        for _, d in DILATED_PATTERNS:
            out_shape.append(sd((bsz, d, seq_len // d, 3 * ATT_W), BF16))
            out_specs.append(pl.BlockSpec((1, d, tm // d, 3 * ATT_W), lambda i: (i // nt, 0, i % nt, 0)))
        out_shape += [sd((depth, bsz, ATT_W, seq_len), F32)] * 2
        out_specs += [pl.BlockSpec((1, 1, ATT_W, tm), lambda i: (layer, i // nt, 0, i % nt))] * 2
        if kv_prev is not None:
            in_specs += [pl.BlockSpec(memory_space=pl.ANY)] * 2
            args += list(kv_prev)
            aliases = {len(args) - 2: len(out_shape) - 2, len(args) - 1: len(out_shape) - 1}
    else:
        out_shape += [sd((n, ATT_W), F32)] * 3
        out_specs += [row(ATT_W)] * 3
    n_in = len(args)

    def body(*refs):
        _proj_kernel(seq_len is not None, *refs[:9], *refs[n_in:])

    return pl.pallas_call(
        body, grid=(n // tm,), in_specs=in_specs, out_specs=out_specs, out_shape=out_shape,
        scratch_shapes=scratch, input_output_aliases=aliases, name="proj", compiler_params=_params("parallel"),
    )(*args)


class _Recurrence:
    def __init__(self, heads, dk, dv, in_refs, out_refs, s_scr):
        self.heads, self.dk, self.dv = heads, dk, dv
        self.q_ref, self.k_ref, self.v_ref, self.g_ref, self.gate_ref, self.nw_ref = in_refs[:6]
        self.s0_ref = in_refs[6] if len(in_refs) > 6 else None
        self.o_ref, self.s_out_ref = out_refs
        self.s_scr = s_scr
        self.group, self.t_tile, self.kwp = self.q_ref.shape
        self.kw, self.vw = heads * dk, heads * dv
        per_k = LANES // dk
        self.per_v = LANES // dv
        kwp, vw = self.kwp, self.vw
        self.tri = (_iota((CHUNK, CHUNK), 0) >= _iota((CHUNK, CHUNK), 1)).astype(BF16)
        self.causal = [(_iota((per_k * HALF, CHUNK), 0) % HALF) + h * HALF >= _iota((per_k * HALF, CHUNK), 1)
                       for h in range(2)]
        self.own_head = ((_iota((per_k * HALF, LANES), 0) // HALF)
                         == (_iota((per_k * HALF, LANES), 1) // dk)).astype(BF16)
        self.bd_mask = (_iota((vw, kwp), 0) // dv) == (_iota((vw, kwp), 1) // dk)
        self.pool = jnp.where((_iota((vw, vw), 0) // dv) == (_iota((vw, vw), 1) // dv), 1.0 / dv, 0.0).astype(BF16)
        self.v_head = _iota((CHUNK, LANES), 1) // dv
        self.k_groups = [(kg, min(per_k, heads - kg * per_k)) for kg in range(kwp // LANES) if heads > kg * per_k]
        self.short = self.t_tile < CHUNK

    def init_state(self):
        for g in range(self.group):
            if self.s0_ref is None:
                self.s_scr[g] = jnp.zeros((self.vw, self.kwp), F32)
                continue
            s_nat = self.s0_ref[g]
            if self.kwp > self.kw:
                s_nat = jnp.concatenate([s_nat, jnp.zeros((self.kwp - self.kw, self.dv), F32)], axis=0)
            tiled = jnp.concatenate([s_nat] * self.heads, axis=1)
            self.s_scr[g] = jnp.where(self.bd_mask, tiled.T, 0.0)

    def write_state(self):
        for g in range(self.group):
            s_bd = self.s_scr[g].T
            s_nat = s_bd[:, 0:self.dv]
            for h in range(1, self.heads):
                s_nat = s_nat + s_bd[:, h * self.dv:(h + 1) * self.dv]
            self.s_out_ref[g] = s_nat[0:self.kw]

    def _load(self, ref, g, rows):
        if not self.short:
            return ref[g, rows, :]
        x = ref[g].astype(F32)
        return jnp.concatenate([x, jnp.zeros((CHUNK - self.t_tile, x.shape[1]), F32)], axis=0)

    @staticmethod
    def _split_dot(a, x, terms):
        acc = None
        for _ in range(terms):
            hi = x.astype(BF16)
            part = _dot(a, hi)
            acc = part if acc is None else acc + part
            x = x - hi.astype(F32)
        return acc

    def chunk_stages(self, rows):
        seqs = range(self.group)
        load, s_scr = self._load, self.s_scr
        b = [self._split_dot(self.tri, load(self.g_ref, g, rows), 2) for g in seqs]
        yield
        q_in, k_in, q_st, k_st, dec = [], [], [], [], []
        for g in seqs:
            q = load(self.q_ref, g, rows).astype(F32)
            k = load(self.k_ref, g, rows).astype(F32)
            b_half, b_last = b[g][HALF - 1:HALF, :], b[g][CHUNK - 1:CHUNK, :]
            refs_g = (0.5 * b_half, 0.5 * (b_half + b_last))
            q_in.append([(q[h * HALF:(h + 1) * HALF] * jnp.exp2(jnp.minimum(
                b[g][h * HALF:(h + 1) * HALF] - refs_g[h], EXP2_CLAMP))).astype(BF16) for h in range(2)])
            k_in.append([(k * jnp.exp2(jnp.minimum(r - b[g], EXP2_CLAMP))).astype(BF16) for r in refs_g])
            q_st.append((q * jnp.exp2(b[g])).astype(BF16))
            k_st.append((k * jnp.exp2(b_last - b[g])).astype(BF16))
            dec.append(jnp.exp2(b_last))
        yield
        a = []
        for g in seqs:
            per_group = []
            for kg, nh in self.k_groups:
                sl = slice(kg * LANES, (kg + 1) * LANES)
                halves = []
                for h in range(2):
                    stack = jnp.concatenate([q_in[g][h][:, sl]] * nh, axis=0) * self.own_head[0:nh * HALF]
                    halves.append(_dot_nt(stack, k_in[g][h][:, sl]))
                per_group.append(halves)
            a.append(per_group)
        yield
        s_old = [s_scr[g] for g in seqs]
        o_inter = [_dot_nt(q_st[g], s_old[g].astype(BF16)) for g in seqs]
        v = [load(self.v_ref, g, rows).astype(BF16) for g in seqs]
        u = [_dot_tn(v[g], k_st[g]) for g in seqs]
        yield
        for g in seqs:
            s_scr[g] = s_old[g] * dec[g] + jnp.where(self.bd_mask, u[g], 0.0)
        o = []
        for g in seqs:
            a_heads = []
            for (kg, nh), ag in zip(self.k_groups, a[g]):
                am = [jnp.where(self.causal[h][0:nh * HALF], ag[h], 0.0).astype(BF16) for h in range(2)]
                a_heads += [jnp.concatenate([am[h][e * HALF:(e + 1) * HALF] for h in range(2)], axis=0)
                            for e in range(nh)]
            pieces = []
            for j in range(self.vw // LANES):
                v_grp = v[g][:, j * LANES:(j + 1) * LANES]
                acc = None
                for e in range(self.per_v):
                    oe = _dot(a_heads[j * self.per_v + e], v_grp)
                    acc = oe if acc is None else jnp.where(self.v_head == e, oe, acc)
                pieces.append(acc)
            o.append(o_inter[g] + jnp.concatenate(pieces, axis=1))
        yield
        ms_all = _dot(jnp.concatenate([(o[g] * o[g]).astype(BF16) for g in seqs], axis=0), self.pool)
        yield
        nw = self.nw_ref[...]
        for g in seqs:
            gate = load(self.gate_ref, g, rows).astype(F32)
            out = (o[g] * lax.rsqrt(ms_all[g * CHUNK:(g + 1) * CHUNK] + EPS) * nw * gate).astype(self.o_ref.dtype)
            if self.short:
                self.o_ref[g] = out[0:self.t_tile]
            else:
                self.o_ref[g, rows, :] = out


def _recur_kernel(shapes, has_init, *refs):
    n_in = 7 if has_init else 6
    n = len(shapes)
    probs = [_Recurrence(*shp, refs[i * n_in:(i + 1) * n_in], refs[n * n_in + 2 * i:n * n_in + 2 * i + 2],
                         refs[n * (n_in + 2) + i]) for i, shp in enumerate(shapes)]
    t_idx = pl.program_id(1)

    @pl.when(t_idx == 0)
    def _():
        for p in probs:
            p.init_state()

    def step(c, carry):
        rows = pl.ds(pl.multiple_of(c * CHUNK, CHUNK), CHUNK)
        for _ in itertools.zip_longest(*[p.chunk_stages(rows) for p in probs]):
            pass
        return carry

    lax.fori_loop(0, max(1, probs[0].t_tile // CHUNK), step, 0)

    @pl.when(t_idx == pl.num_programs(1) - 1)
    def _():
        for p in probs:
            p.write_state()


def _recur_call(problems, layer, name):
    bsz, t_len, _ = problems[0][0].shape
    group = math.gcd(RECUR_GROUP, bsz)
    tt = min(RECUR_TT, t_len)
    has_init = problems[0][6] is not None
    seq = lambda w: pl.BlockSpec((group, tt, w), lambda b, t: (b, t, 0))
    in_specs, args, out_specs, out_shape, scratch, shapes = [], [], [], [], [], []
    for q, k, v, g, gate, nw, s0, heads, dk, dv in problems:
        kwp, kw, vw = q.shape[2], heads * dk, heads * dv
        st = pl.BlockSpec((group, kw, dv), lambda b, t: (b, 0, 0))
        in_specs += [seq(kwp), seq(kwp), seq(vw), seq(kwp), seq(vw), _layer_spec((1, vw), layer)]
        args += [q, k, v, g, gate, nw]
        if has_init:
            in_specs.append(st)
            args.append(s0)
        out_specs += [seq(vw), st]
        out_shape += [jax.ShapeDtypeStruct((bsz, t_len, vw), BF16), jax.ShapeDtypeStruct((bsz, kw, dv), F32)]
        scratch.append(pltpu.VMEM((group, vw, kwp), F32))
        shapes.append((heads, dk, dv))
    outs = pl.pallas_call(
        functools.partial(_recur_kernel, tuple(shapes), has_init),
        grid=(bsz // group, t_len // tt), in_specs=in_specs, out_specs=out_specs, out_shape=out_shape,
        scratch_shapes=scratch, name=name, compiler_params=_params("parallel", "arbitrary"),
    )(*args)
    return [tuple(outs[2 * i:2 * i + 2]) for i in range(len(problems))]


def _log_multiplicity(delta):
    delta = np.asarray(delta, np.int64)
    cnt = np.zeros(delta.shape, np.float64)
    for w, d in DILATED_PATTERNS:
        cnt += (delta >= 0) & (delta <= w) & (delta % d == 0)
    return np.where(cnt > 0, np.log2(np.maximum(cnt, 1.0)), NEG).astype(np.float32)


ATTN_UNITS_PER_GROUP = 16


def _prompt_attn_kernel(*refs):
    n_pat = len(DILATED_PATTERNS)
    qs, ks, vs = refs[0:3 * n_pat:3], refs[1:3 * n_pat:3], refs[2:3 * n_pat:3]
    mask_ref, o_ref = refs[3 * n_pat:3 * n_pat + 2]
    out_scr, *scr = refs[3 * n_pat + 2:]
    nd, md, ld = scr[0::3], scr[1::3], scr[2::3]
    t_len = o_ref.shape[1]
    dils = [d for _, d in DILATED_PATTERNS]
    lo = _iota((QBLK, LANES), 1) < HEAD_DIM

    units = []
    for di, d in enumerate(dils):
        for r in range(d):
            for pb in range(t_len // d // QBLK):
                q0 = pb * QBLK
                k0, klen = (q0, QBLK) if pb == 0 else (q0 - QBLK, 2 * QBLK)
                units.append((di, r, q0, k0, klen))

    eye = (_iota((QBLK, QBLK), 0) == _iota((QBLK, QBLK), 1)).astype(BF16)
    mask_t = mask_ref[...]

    def run(group):
        chains = [(u, e) for u in group for e in (0, 1)]
        scores = []
        for (di, r, q0, k0, klen), e in chains:
            q = qs[di][0, r, q0:q0 + QBLK, :]
            q = jnp.where(lo, q, jnp.zeros_like(q)) if e == 0 else jnp.where(lo, jnp.zeros_like(q), q)
            k_aug = jnp.concatenate([ks[di][0, r, k0:k0 + klen, :], mask_t[2 * QBLK - klen:2 * QBLK]], axis=1)
            scores.append(_dot_nt(jnp.concatenate([q, eye], axis=1), k_aug))
        for i, (di, r, q0, k0, klen) in enumerate(group):
            s = scores[2 * i:2 * i + 2]
            m = [jnp.max(x, axis=1, keepdims=True) for x in s]
            p = [jnp.exp2(x - mx) for x, mx in zip(s, m)]
            l = [jnp.sum(x, axis=1, keepdims=True) for x in p]
            num = [_dot(x.astype(BF16), vs[di][0, r, k0:k0 + klen, :]) for x in p]
            n_u = jnp.where(lo, num[0], num[1])
            m_u = jnp.where(lo, m[0], m[1])
            l_u = jnp.where(lo, l[0], l[1])
            if di < n_pat - 1:
                d = dils[di]
                f = d_max // d
                for dst, val in ((nd[di], n_u), (md[di], m_u), (ld[di], l_u)):
                    by_res = jnp.swapaxes(val.reshape(QBLK // f, f, LANES), 0, 1)
                    for c in range(f):
                        dst[d * c + r, q0 // f:(q0 + QBLK) // f, :] = by_res[c]
                continue
            ms = [mr[r] for mr in md] + [m_u]
            top = functools.reduce(jnp.maximum, ms)
            w = [jnp.exp2(x - top) for x in ms]
            acc = sum(wx * nx for wx, nx in zip(w, [nr[r] for nr in nd] + [n_u]))
            den = sum(wx * lx for wx, lx in zip(w, [lr[r] for lr in ld] + [l_u]))
            out_scr[r] = acc * (1.0 / den)

    d_max = dils[-1]
    for i in range(0, len(units), ATTN_UNITS_PER_GROUP):
        run(units[i:i + ATTN_UNITS_PER_GROUP])
    rows_per = t_len // d_max // 4
    for blk in range(4):
        x = out_scr[:, blk * rows_per:(blk + 1) * rows_per, :]
        o_ref[0, blk * rows_per * d_max:(blk + 1) * rows_per * d_max, :] = (
            jnp.swapaxes(x, 0, 1).reshape(rows_per * d_max, LANES).astype(o_ref.dtype))


def _prompt_attn_call(streams, t_len):
    bsz = streams[0].shape[0]
    dils = [d for _, d in DILATED_PATTERNS]
    assert all(w // d == QBLK for w, d in DILATED_PATTERNS) and t_len == QBLK * dils[-1] == QBLK * max(dils)
    a = np.arange(QBLK)[:, None]
    c = np.arange(2 * QBLK)[None, :]
    mask_t = jnp.asarray(np.where((c >= a) & (c <= a + QBLK), 0.0, NEG).astype(np.float32).T, dtype=BF16)
    n_grp = ATT_W // LANES
    in_specs, args = [], []
    for s_d in streams:
        d, n = s_d.shape[1], s_d.shape[2]
        for part in range(3):
            in_specs.append(pl.BlockSpec((1, d, n, LANES), lambda b, j, part=part: (b, 0, 0, part * n_grp + j)))
            args.append(s_d)
    by_residue = pltpu.VMEM((dils[-1], t_len // dils[-1], LANES), F32)
    scratch = [by_residue] * (1 + 3 * (len(dils) - 1))
    return pl.pallas_call(
        _prompt_attn_kernel, grid=(bsz, n_grp),
        in_specs=in_specs + [_const_spec((2 * QBLK, QBLK))],
        out_specs=pl.BlockSpec((1, t_len, LANES), lambda b, j: (b, 0, j)),
        out_shape=jax.ShapeDtypeStruct((bsz, t_len, ATT_W), BF16), name="prompt_attn",
        scratch_shapes=scratch,
        compiler_params=_params("parallel", "parallel"),
    )(*args, mask_t)


def _sample_attn_kernel(with_prev, *refs):
    if with_prev:
        q_ref, kn_ref, vn_ref, ck_ref, cv_ref, bias_ref, _, _, o_ref, ok_ref, ov_ref, kb_scr, vb_scr = refs
    else:
        q_ref, kn_ref, vn_ref, ck_ref, cv_ref, bias_ref, o_ref, ok_ref, ov_ref, kb_scr, vb_scr = refs
    win = ck_ref.shape[3]
    t_new = kn_ref.shape[1]
    tail = _iota((LANES, LANES), 1) >= LANES - t_new
    for c_ref, n_ref, out_ref, scr in ((ck_ref, kn_ref, ok_ref, kb_scr), (cv_ref, vn_ref, ov_ref, vb_scr)):
        new_t = jnp.concatenate([n_ref[0], jnp.zeros((LANES - t_new, ATT_W), F32)], axis=0).T
        for rb in range(ATT_W // LANES):
            rows = slice(rb * LANES, (rb + 1) * LANES)
            old = c_ref[0, 0, rows, :]
            shifted = pltpu.roll(old, win - t_new, 1)
            out_ref[0, 0, rows, 0:win - LANES] = shifted[:, 0:win - LANES]
            out_ref[0, 0, rows, win - LANES:win] = jnp.where(
                tail, pltpu.roll(new_t[rows], LANES - t_new, 1), shifted[:, win - LANES:win])
            scr[rows, 0:win] = old.astype(BF16)
            scr[rows, win:win + LANES] = new_t[rows].astype(BF16)
    q = q_ref[0].astype(F32)
    rows = ATT_HEADS * t_new
    own = (_iota((rows, ATT_W), 0) // t_new) == (_iota((rows, ATT_W), 1) // HEAD_DIM)
    q_stack = jnp.where(own, jnp.concatenate([q] * ATT_HEADS, axis=0), 0.0).astype(BF16)
    s = _dot(q_stack, kb_scr[...]) + bias_ref[...]
    m = jnp.max(s, axis=1, keepdims=True)
    p = jnp.exp2(s - m)
    l = jnp.sum(p, axis=1, keepdims=True)
    o_all = jnp.where(own, _dot_nt(p.astype(BF16), vb_scr[...]) * (1.0 / l), 0.0)
    o = o_all[0:t_new]
    for h in range(1, ATT_HEADS):
        o = o + o_all[h * t_new:(h + 1) * t_new]
    o_ref[0] = o.astype(o_ref.dtype)


def _sample_attn_call(q, k_new, v_new, cache_kt, cache_vt, prev, layer):
    depth, bsz, _, win = cache_kt.shape
    t_new = q.shape[1]
    rows = ATT_HEADS * t_new
    tq = np.arange(rows)[:, None] % t_new
    n = np.arange(win + LANES)[None, :]
    bias = np.where(n < win + t_new, _log_multiplicity(win + tq - n), NEG).astype(np.float32)
    new = pl.BlockSpec((1, t_new, ATT_W), lambda b: (b, 0, 0))
    cache = pl.BlockSpec((1, 1, ATT_W, win), lambda b: (layer, b, 0, 0))
    in_specs = [new, new, new, cache, cache, _const_spec((rows, win + LANES))]
    args = [q, k_new, v_new, cache_kt, cache_vt, jnp.asarray(bias)]
    aliases = {}
    if prev is not None:
        in_specs += [pl.BlockSpec(memory_space=pl.ANY)] * 2
        args += list(prev)
        aliases = {6: 1, 7: 2}
    sd = jax.ShapeDtypeStruct
    return pl.pallas_call(
        functools.partial(_sample_attn_kernel, prev is not None), grid=(bsz,),
        in_specs=in_specs, out_specs=[new, cache, cache],
        out_shape=[sd((bsz, t_new, ATT_W), BF16), sd(cache_kt.shape, F32), sd(cache_vt.shape, F32)],
        scratch_shapes=[pltpu.VMEM((ATT_W, win + LANES), BF16)] * 2,
        input_output_aliases=aliases, name="sample_attn", compiler_params=_params("parallel"),
    )(*args)


FF_CHUNK = 1024


def _mlp_kernel(h_ref, oh_ref, oa_ref, og_ref, p_ref, an_ref, wo_ref, npm_ref, npre_ref, wup_ref, wdn_ref,
                npost_ref, wpg_ref, wple_ref, out_ref):
    half = h_ref.shape[0] // 2
    halves = [slice(0, half), slice(half, 2 * half)]
    oa = [(_rms(oa_ref[s, :].astype(F32)) * an_ref[...]).astype(BF16) for s in halves]
    mix = [_dot(jnp.concatenate([oh_ref[s, :], oa[i], og_ref[s, :]], axis=1), wo_ref[...])
           for i, s in enumerate(halves)]
    ple = [_dot(p_ref[s, :].astype(BF16), wple_ref[...]) for s in halves]
    h = [h_ref[s, :] + _rms(mix[i]) * npm_ref[...] for i, s in enumerate(halves)]
    xn = [(_rms(x) * npre_ref[...]).astype(BF16) for x in h]
    acc = [None, None]
    for c in range(D_FF // FF_CHUNK):
        cols = slice(c * FF_CHUNK, (c + 1) * FF_CHUNK)
        u = [jnp.maximum(_dot(x, wup_ref[:, cols]), 0.0) for x in xn]
        part = [_dot((x * x).astype(BF16), wdn_ref[cols, :]) for x in u]
        acc = [p if a is None else a + p for a, p in zip(acc, part)]
    h = [x + _rms(a) * npost_ref[...] for x, a in zip(h, acc)]
    gate = [_sigmoid(_dot(x.astype(BF16), wpg_ref[...])) for x in h]
    for i, s in enumerate(halves):
        out_ref[s, :] = h[i] + gate[i] * ple[i]


def _mlp_call(h2d, oh, oa, og, p3d, layer, an, wo, npm, npre, wup, wdn, npost, wpg, wple, tm):
    n = h2d.shape[0]
    row = lambda w: pl.BlockSpec((tm, w), lambda i: (i, 0))
    lspec = lambda *shape: _layer_spec(shape, layer)
    vec = lspec(1, D_MODEL)
    return pl.pallas_call(
        _mlp_kernel, grid=(n // tm,),
        in_specs=[row(D_MODEL), row(HG_W), row(ATT_W), row(GLA_VW),
                  pl.BlockSpec((None, tm, D_PLE), lambda i: (layer, i, 0)), lspec(1, ATT_W),
                  lspec(D_MODEL, D_MODEL), vec, vec, lspec(D_MODEL, D_FF), lspec(D_FF, D_MODEL), vec,
                  lspec(D_MODEL, D_MODEL), lspec(D_PLE, D_MODEL)],
        out_specs=row(D_MODEL), out_shape=jax.ShapeDtypeStruct((n, D_MODEL), F32), name="mlp",
        compiler_params=_params("parallel"),
    )(h2d, oh, oa, og, p3d, an, wo, npm, npre, wup, wdn, npost, wpg, wple)


def _pack_w_in(w_in):
    splits = np.cumsum([0, HG_W, HG_W, HG_W, HG_W, ATT_W, ATT_W, ATT_W, GLA_KW, GLA_KW, GLA_VW,
                        GLA_GATE_RANK, GLA_VW])
    col = lambda i: w_in[..., splits[i]:splits[i + 1]]
    padc = lambda a, w: jnp.pad(a, ((0, 0), (0, 0), (0, w - a.shape[-1])))
    parts = [col(0), col(1), col(2), col(3), col(4), col(5), col(6),
             padc(col(7), GLA_KP), padc(col(8), GLA_KP), col(9), col(11), padc(col(10), LANES)]
    return jnp.concatenate(parts, axis=-1).astype(BF16)


def _rope_tables(pos):
    half = ROT_DIM // 2
    inv = jnp.exp(-math.log(ROPE_THETA) * jnp.arange(half, dtype=F32) * (2.0 / ROT_DIM))
    ang = pos[:, None] * inv[None, :]
    cos, sin = jnp.cos(ang), jnp.sin(ang)
    d = np.arange(LANES) % HEAD_DIM
    first, second = d < half, (d >= half) & (d < ROT_DIM)
    idx = np.where(second, d - half, np.where(first, d, 0))
    cos_t = jnp.where(first | second, cos[:, idx], 1.0)
    sin_a = jnp.where(first, -sin[:, idx], 0.0)
    sin_b = jnp.where(second, sin[:, idx], 0.0)
    return cos_t, sin_a, sin_b


def kernel(x_prompt, x_sample, state_hgrn, state_gla, cache_k, cache_v, p_prompt, p_sample, norm_pre_mix, w_in, hgrn_lb, hgrn_norm, attn_norm, gla_w_gate2, gla_b_gate, gla_norm, w_out, norm_post_mix, norm_pre_mlp, w_up, w_down, norm_post_mlp, w_ple_gate, w_ple):
    depth = w_in.shape[0]
    bp, tp, _ = x_prompt.shape
    bs, ts, _ = x_sample.shape
    win = cache_k.shape[2]

    lb_cum = jnp.cumsum(jax.nn.softmax(hgrn_lb.astype(F32), axis=0), axis=0)
    lower_bounds = lb_cum - lb_cum[0:1]

    tm_p = min(TOKEN_TILE, tp)
    tm_s = bs * ts
    rope_p = _rope_tables(jnp.arange(tp, dtype=F32))
    rope_s = tuple(jnp.tile(t, (bs, 1)) for t in _rope_tables(jnp.arange(ts, dtype=F32) + PAST_LEN))
    to_t = lambda a: jnp.transpose(a, (0, 1, 3, 4, 2)).reshape(depth, a.shape[1], ATT_W, a.shape[2])
    from_t = lambda a: jnp.transpose(a.reshape(depth, a.shape[1], ATT_HEADS, HEAD_DIM, a.shape[3]), (0, 1, 4, 2, 3))
    ck_t, cv_t = to_t(cache_k), to_t(cache_v)
    pp3 = p_prompt.reshape(depth, bp * tp, D_PLE)
    ps3 = p_sample.reshape(depth, bs * ts, D_PLE)

    hp = x_prompt.reshape(bp * tp, D_MODEL)
    hs = x_sample.reshape(bs * ts, D_MODEL)
    kv_p = None
    kv_s = None
    hg_p, gl_p, hg_s, gl_s = [], [], [], []
    row = lambda a: a.reshape(depth, 1, -1)
    w_all = _pack_w_in(w_in)
    wg2 = jnp.pad(gla_w_gate2, ((0, 0), (0, LANES - GLA_GATE_RANK), (0, GLA_KP - GLA_KW))).astype(BF16)
    bg = row(jnp.pad(gla_b_gate, ((0, 0), (0, GLA_KP - GLA_KW))))
    proj_w = (row(norm_pre_mix), w_all, row(lower_bounds))
    mlp_w = (row(attn_norm), w_out.astype(BF16), row(norm_post_mix), row(norm_pre_mlp), w_up.astype(BF16),
             w_down.astype(BF16), row(norm_post_mlp), w_ple_gate.astype(BF16), w_ple.astype(BF16))
    hgrn_nw, gla_nw = row(hgrn_norm), row(gla_norm)
    for i in range(depth):
        (hq, hk, hv, hlf, hgt, gq, gk, gv, gla, ggt, *att_streams, k_t, v_t) = _proj_call(
            hp, *proj_w, rope_p, wg2, bg, kv_p, i, i, depth, tm_p, tp)
        kv_p = (k_t, v_t)
        sh = lambda a: a.reshape(bp, tp, a.shape[-1])
        (oh, s_h), (og, s_g) = _recur_call(
            [(sh(hq), sh(hk), sh(hv), sh(hlf), sh(hgt), hgrn_nw, None, HG_HEADS, HG_DK, HG_DV),
             (sh(gq), sh(gk), sh(gv), sh(gla), sh(ggt), gla_nw, None, GLA_HEADS, GLA_DK, GLA_DV)], i, "recur")
        oa = _prompt_attn_call(att_streams, tp)
        fl = lambda a: a.reshape(bp * tp, a.shape[-1])
        hp = _mlp_call(hp, fl(oh), fl(oa), fl(og), pp3, i, *mlp_w, tm_p)
        hg_p.append(s_h.reshape(bp, HG_HEADS, HG_DK, HG_DV))
        gl_p.append(s_g.reshape(bp, GLA_HEADS, GLA_DK, GLA_DV))

        (hq, hk, hv, hlf, hgt, gq, gk, gv, gla, ggt, aq, k_new, v_new) = _proj_call(
            hs, *proj_w, rope_s, wg2, bg, None, i, 0, 1, tm_s, None)
        pt = lambda a: a.reshape(bs, ts, a.shape[-1])
        (oh, s_h), (og, s_g) = _recur_call(
            [(pt(hq), pt(hk), pt(hv), pt(hlf), pt(hgt), hgrn_nw, state_hgrn[i].reshape(bs, HG_W, HG_DV),
              HG_HEADS, HG_DK, HG_DV),
             (pt(gq), pt(gk), pt(gv), pt(gla), pt(ggt), gla_nw, state_gla[i].reshape(bs, GLA_KW, GLA_DV),
              GLA_HEADS, GLA_DK, GLA_DV)], i, "recur_s")
        s3 = lambda a: a.reshape(bs, ts, ATT_W)
        oa, ck_new, cv_new = _sample_attn_call(s3(aq), s3(k_new), s3(v_new), ck_t, cv_t, kv_s, i)
        kv_s = (ck_new, cv_new)
        ut = lambda a: a.reshape(bs * ts, a.shape[-1])
        hs = _mlp_call(hs, ut(oh), oa.reshape(bs * ts, ATT_W), ut(og), ps3, i, *mlp_w, tm_s)
        hg_s.append(s_h.reshape(bs, HG_HEADS, HG_DK, HG_DV))
        gl_s.append(s_g.reshape(bs, GLA_HEADS, GLA_DK, GLA_DV))

    return (hp.reshape(bp, tp, D_MODEL), hs.reshape(bs, ts, D_MODEL),
            jnp.stack(hg_p), jnp.stack(gl_p), from_t(kv_p[0]), from_t(kv_p[1]),
            jnp.stack(hg_s), jnp.stack(gl_s), from_t(kv_s[0]), from_t(kv_s[1]))
```

```python
import functools
import itertools
import math

import jax
import jax.numpy as jnp
import numpy as np
from jax import lax
from jax.experimental import pallas as pl
from jax.experimental.pallas import tpu as pltpu

F32 = jnp.float32
BF16 = jnp.bfloat16

D_MODEL = 1024
HEAD_DIM = 64
HG_HEADS, HG_DK, HG_DV = 4, 64, 64
ATT_HEADS = 6
GLA_HEADS, GLA_DK, GLA_DV = 6, 32, 64
GLA_GATE_RANK = 16
GLA_TAU = 16.0
D_FF = 4 * D_MODEL
D_PLE = 256
ROPE_THETA = 500000.0
ROT_DIM = HEAD_DIM // 4
DILATED_PATTERNS = ((128, 1), (512, 4), (2048, 16))
PAST_LEN = 16384
EPS = 1e-6

LANES = 128
HG_W = HG_HEADS * HG_DK
ATT_W = ATT_HEADS * HEAD_DIM
GLA_KW = GLA_HEADS * GLA_DK
GLA_KP = 256
GLA_VW = GLA_HEADS * GLA_DV

CHUNK = 64
HALF = CHUNK // 2
TOKEN_TILE = 512
RECUR_GROUP = 8
RECUR_TT = 256
QBLK = 128
NEG = -1e30
LOG2_E = math.log2(math.e)
EXP2_CLAMP = 115.0
Q_SCALE = HEAD_DIM ** -0.5 * LOG2_E
VMEM_LIMIT = 56 * 1024 * 1024

C_HG = 0
C_AT = C_HG + 4 * HG_W
C_GL = C_AT + 3 * ATT_W
GL_Q, GL_K, GL_V, GL_R, GL_LR = 0, 256, 512, 896, 1280
GL_COLS = 1408
W_COLS = C_GL + GL_COLS


def _rms(x):
    return x * lax.rsqrt(jnp.mean(x * x, axis=-1, keepdims=True) + EPS)


def _sigmoid(x):
    return 1.0 / (1.0 + jnp.exp(-x))


def _dot(a, b):
    return jnp.dot(a, b, preferred_element_type=F32)


def _dot_nt(a, b):
    return lax.dot_general(a, b, (((1,), (1,)), ((), ())), preferred_element_type=F32)


def _dot_tn(a, b):
    return lax.dot_general(a, b, (((0,), (0,)), ((), ())), preferred_element_type=F32)


def _iota(shape, d):
    return lax.broadcasted_iota(jnp.int32, shape, d)


def _const_spec(shape):
    nd = len(shape)
    return pl.BlockSpec(shape, lambda *_: (0,) * nd, pipeline_mode=pl.Buffered(1))


def _layer_spec(shape, layer):
    nd = len(shape)
    return pl.BlockSpec((None,) + tuple(shape), lambda *_: (layer,) + (0,) * nd, pipeline_mode=pl.Buffered(1))


def _params(*semantics):
    return pltpu.CompilerParams(dimension_semantics=semantics, vmem_limit_bytes=VMEM_LIMIT)


def _proj_kernel(streams, x_ref, gpre_ref, w_ref, lb_ref, cos_ref, sa_ref, sb_ref, wg2_ref, bg_ref,
                 hq_o, hk_o, hv_o, hlf_o, hgt_o, gq_o, gk_o, gv_o, gla_o, ggt_o, *att_refs):
    xn = (_rms(x_ref[...]) * gpre_ref[...]).astype(BF16)

    y_at = _dot(xn, w_ref[:, C_AT:C_AT + 3 * ATT_W])
    y_hg = _dot(xn, w_ref[:, C_HG:C_HG + 4 * HG_W])
    y_gl = _dot(xn, w_ref[:, C_GL:C_GL + GL_COLS])

    y = y_at
    cos_t, sin_a, sin_b = cos_ref[...], sa_ref[...], sb_ref[...]

    def rope(v):
        return v * cos_t + pltpu.roll(v, LANES - ROT_DIM // 2, 1) * sin_a + pltpu.roll(v, ROT_DIM // 2, 1) * sin_b

    if streams:
        *stream_os, ak_o, av_o = att_refs
    else:
        aq_o, ak_o, av_o = att_refs
    for j in range(ATT_W // LANES):
        sl = slice(j * LANES, (j + 1) * LANES)
        q_rot = rope(y[:, sl]) * Q_SCALE
        k_rot = rope(y[:, ATT_W + j * LANES:ATT_W + (j + 1) * LANES])
        v_grp = y[:, 2 * ATT_W + j * LANES:2 * ATT_W + (j + 1) * LANES]
        if streams:
            ak_o[0, 0, sl, :] = k_rot.T
            av_o[0, 0, sl, :] = v_grp.T
            for s_o in stream_os:
                d, n = s_o.shape[1], s_o.shape[2]
                for part, x in enumerate((q_rot, k_rot, v_grp)):
                    x = jnp.swapaxes(x.reshape(n, d, LANES), 0, 1) if d > 1 else x.reshape(1, n, LANES)
                    lanes = slice(part * ATT_W + j * LANES, part * ATT_W + (j + 1) * LANES)
                    s_o[0, :, :, lanes] = x.astype(BF16)
        else:
            aq_o[:, sl] = q_rot
            ak_o[:, sl] = k_rot
            av_o[:, sl] = v_grp

    y = y_hg
    lb = lb_ref[...]
    f = lb + (1.0 - lb) * _sigmoid(y[:, HG_W:2 * HG_W])
    hg = y[:, 3 * HG_W:4 * HG_W]
    hq_o[...] = y[:, 0:HG_W]
    hk_o[...] = 1.0 - f
    hv_o[...] = y[:, 2 * HG_W:3 * HG_W].astype(BF16)
    hlf_o[...] = jnp.log2(f)
    hgt_o[...] = (hg * _sigmoid(hg)).astype(BF16)

    y = y_gl
    gr = y[:, GL_R:GL_R + GLA_VW]
    z = _dot(y[:, GL_LR:GL_LR + LANES].astype(BF16), wg2_ref[...]) + bg_ref[...]
    log_a = (jnp.minimum(z, 0.0) - jnp.log(1.0 + jnp.exp(-jnp.abs(z)))) * (LOG2_E / GLA_TAU)
    gq_o[...] = y[:, GL_Q:GL_Q + GLA_KP] * (GLA_DK ** -0.5)
    gk_o[...] = y[:, GL_K:GL_K + GLA_KP]
    gv_o[...] = y[:, GL_V:GL_V + GLA_VW].astype(BF16)
    gla_o[...] = log_a
    ggt_o[...] = (gr * _sigmoid(gr)).astype(BF16)


def _proj_call(x2d, gpre, w_all, lb, rope_tabs, wg2, bg, kv_prev, w_layer, layer, depth, tm, seq_len):
    n = x2d.shape[0]
    cos_t, sin_a, sin_b = rope_tabs
    tab_blocks = cos_t.shape[0] // tm
    row = lambda w: pl.BlockSpec((tm, w), lambda i: (i, 0))
    tab = pl.BlockSpec((tm, LANES), lambda i: (i % tab_blocks, 0))
    sd = jax.ShapeDtypeStruct
    lspec = lambda *shape: _layer_spec(shape, w_layer)
    in_specs = [row(D_MODEL), lspec(1, D_MODEL), lspec(D_MODEL, W_COLS), lspec(1, HG_W), tab, tab, tab,
                lspec(LANES, GLA_KP), lspec(1, GLA_KP)]
    args = [x2d, gpre, w_all, lb, cos_t, sin_a, sin_b, wg2, bg]
    out_shape = [sd((n, HG_W), F32), sd((n, HG_W), F32), sd((n, HG_W), BF16), sd((n, HG_W), F32),
                 sd((n, HG_W), BF16), sd((n, GLA_KP), F32), sd((n, GLA_KP), F32),
                 sd((n, GLA_VW), BF16), sd((n, GLA_KP), F32), sd((n, GLA_VW), BF16)]
    out_specs = [row(HG_W)] * 5 + [row(GLA_KP), row(GLA_KP), row(GLA_VW), row(GLA_KP), row(GLA_VW)]
    aliases, scratch = {}, []
    if seq_len is not None:
        nt, bsz = seq_len // tm, n // seq_len
        for _, d in DILATED_PATTERNS:
            out_shape.append(sd((bsz, d, seq_len // d, 3 * ATT_W), BF16))
            out_specs.append(pl.BlockSpec((1, d, tm // d, 3 * ATT_W), lambda i: (i // nt, 0, i % nt, 0)))
        out_shape += [sd((depth, bsz, ATT_W, seq_len), F32)] * 2
        out_specs += [pl.BlockSpec((1, 1, ATT_W, tm), lambda i: (layer, i // nt, 0, i % nt))] * 2
        if kv_prev is not None:
            in_specs += [pl.BlockSpec(memory_space=pl.ANY)] * 2
            args += list(kv_prev)
            aliases = {len(args) - 2: len(out_shape) - 2, len(args) - 1: len(out_shape) - 1}
    else:
        out_shape += [sd((n, ATT_W), F32)] * 3
        out_specs += [row(ATT_W)] * 3
    n_in = len(args)

    def body(*refs):
        _proj_kernel(seq_len is not None, *refs[:9], *refs[n_in:])

    return pl.pallas_call(
        body, grid=(n // tm,), in_specs=in_specs, out_specs=out_specs, out_shape=out_shape,
        scratch_shapes=scratch, input_output_aliases=aliases, name="proj", compiler_params=_params("parallel"),
    )(*args)


class _Recurrence:
    def __init__(self, heads, dk, dv, in_refs, out_refs, s_scr):
        self.heads, self.dk, self.dv = heads, dk, dv
        self.q_ref, self.k_ref, self.v_ref, self.g_ref, self.gate_ref, self.nw_ref = in_refs[:6]
        self.s0_ref = in_refs[6] if len(in_refs) > 6 else None
        self.o_ref, self.s_out_ref = out_refs
        self.s_scr = s_scr
        self.group, self.t_tile, self.kwp = self.q_ref.shape
        self.kw, self.vw = heads * dk, heads * dv
        per_k = LANES // dk
        self.per_v = LANES // dv
        kwp, vw = self.kwp, self.vw
        self.tri = (_iota((CHUNK, CHUNK), 0) >= _iota((CHUNK, CHUNK), 1)).astype(BF16)
        self.causal = [(_iota((per_k * HALF, CHUNK), 0) % HALF) + h * HALF >= _iota((per_k * HALF, CHUNK), 1)
                       for h in range(2)]
        self.own_head = ((_iota((per_k * HALF, LANES), 0) // HALF)
                         == (_iota((per_k * HALF, LANES), 1) // dk)).astype(BF16)
        self.bd_mask = (_iota((vw, kwp), 0) // dv) == (_iota((vw, kwp), 1) // dk)
        self.pool = jnp.where((_iota((vw, vw), 0) // dv) == (_iota((vw, vw), 1) // dv), 1.0 / dv, 0.0).astype(BF16)
        self.v_head = _iota((CHUNK, LANES), 1) // dv
        self.k_groups = [(kg, min(per_k, heads - kg * per_k)) for kg in range(kwp // LANES) if heads > kg * per_k]
        self.short = self.t_tile < CHUNK

    def init_state(self):
        for g in range(self.group):
            if self.s0_ref is None:
                self.s_scr[g] = jnp.zeros((self.vw, self.kwp), F32)
                continue
            s_nat = self.s0_ref[g]
            if self.kwp > self.kw:
                s_nat = jnp.concatenate([s_nat, jnp.zeros((self.kwp - self.kw, self.dv), F32)], axis=0)
            tiled = jnp.concatenate([s_nat] * self.heads, axis=1)
            self.s_scr[g] = jnp.where(self.bd_mask, tiled.T, 0.0)

    def write_state(self):
        for g in range(self.group):
            s_bd = self.s_scr[g].T
            s_nat = s_bd[:, 0:self.dv]
            for h in range(1, self.heads):
                s_nat = s_nat + s_bd[:, h * self.dv:(h + 1) * self.dv]
            self.s_out_ref[g] = s_nat[0:self.kw]

    def _load(self, ref, g, rows):
        if not self.short:
            return ref[g, rows, :]
        x = ref[g].astype(F32)
        return jnp.concatenate([x, jnp.zeros((CHUNK - self.t_tile, x.shape[1]), F32)], axis=0)

    @staticmethod
    def _split_dot(a, x, terms):
        acc = None
        for _ in range(terms):
            hi = x.astype(BF16)
            part = _dot(a, hi)
            acc = part if acc is None else acc + part
            x = x - hi.astype(F32)
        return acc

    def chunk_stages(self, rows):
        seqs = range(self.group)
        load, s_scr = self._load, self.s_scr
        b = [self._split_dot(self.tri, load(self.g_ref, g, rows), 2) for g in seqs]
        yield
        q_in, k_in, q_st, k_st, dec = [], [], [], [], []
        for g in seqs:
            q = load(self.q_ref, g, rows).astype(F32)
            k = load(self.k_ref, g, rows).astype(F32)
            b_half, b_last = b[g][HALF - 1:HALF, :], b[g][CHUNK - 1:CHUNK, :]
            refs_g = (0.5 * b_half, 0.5 * (b_half + b_last))
            q_in.append([(q[h * HALF:(h + 1) * HALF] * jnp.exp2(jnp.minimum(
                b[g][h * HALF:(h + 1) * HALF] - refs_g[h], EXP2_CLAMP))).astype(BF16) for h in range(2)])
            k_in.append([(k * jnp.exp2(jnp.minimum(r - b[g], EXP2_CLAMP))).astype(BF16) for r in refs_g])
            q_st.append((q * jnp.exp2(b[g])).astype(BF16))
            k_st.append((k * jnp.exp2(b_last - b[g])).astype(BF16))
            dec.append(jnp.exp2(b_last))
        yield
        a = []
        for g in seqs:
            per_group = []
            for kg, nh in self.k_groups:
                sl = slice(kg * LANES, (kg + 1) * LANES)
                halves = []
                for h in range(2):
                    stack = jnp.concatenate([q_in[g][h][:, sl]] * nh, axis=0) * self.own_head[0:nh * HALF]
                    halves.append(_dot_nt(stack, k_in[g][h][:, sl]))
                per_group.append(halves)
            a.append(per_group)
        yield
        s_old = [s_scr[g] for g in seqs]
        o_inter = [_dot_nt(q_st[g], s_old[g].astype(BF16)) for g in seqs]
        v = [load(self.v_ref, g, rows).astype(BF16) for g in seqs]
        u = [_dot_tn(v[g], k_st[g]) for g in seqs]
        yield
        for g in seqs:
            s_scr[g] = s_old[g] * dec[g] + jnp.where(self.bd_mask, u[g], 0.0)
        o = []
        for g in seqs:
            a_heads = []
            for (kg, nh), ag in zip(self.k_groups, a[g]):
                am = [jnp.where(self.causal[h][0:nh * HALF], ag[h], 0.0).astype(BF16) for h in range(2)]
                a_heads += [jnp.concatenate([am[h][e * HALF:(e + 1) * HALF] for h in range(2)], axis=0)
                            for e in range(nh)]
            pieces = []
            for j in range(self.vw // LANES):
                v_grp = v[g][:, j * LANES:(j + 1) * LANES]
                acc = None
                for e in range(self.per_v):
                    oe = _dot(a_heads[j * self.per_v + e], v_grp)
                    acc = oe if acc is None else jnp.where(self.v_head == e, oe, acc)
                pieces.append(acc)
            o.append(o_inter[g] + jnp.concatenate(pieces, axis=1))
        yield
        ms_all = _dot(jnp.concatenate([(o[g] * o[g]).astype(BF16) for g in seqs], axis=0), self.pool)
        yield
        nw = self.nw_ref[...]
        for g in seqs:
            gate = load(self.gate_ref, g, rows).astype(F32)
            out = (o[g] * lax.rsqrt(ms_all[g * CHUNK:(g + 1) * CHUNK] + EPS) * nw * gate).astype(self.o_ref.dtype)
            if self.short:
                self.o_ref[g] = out[0:self.t_tile]
            else:
                self.o_ref[g, rows, :] = out


def _recur_kernel(shapes, has_init, *refs):
    n_in = 7 if has_init else 6
    n = len(shapes)
    probs = [_Recurrence(*shp, refs[i * n_in:(i + 1) * n_in], refs[n * n_in + 2 * i:n * n_in + 2 * i + 2],
                         refs[n * (n_in + 2) + i]) for i, shp in enumerate(shapes)]
    t_idx = pl.program_id(1)

    @pl.when(t_idx == 0)
    def _():
        for p in probs:
            p.init_state()

    def step(c, carry):
        rows = pl.ds(pl.multiple_of(c * CHUNK, CHUNK), CHUNK)
        for _ in itertools.zip_longest(*[p.chunk_stages(rows) for p in probs]):
            pass
        return carry

    lax.fori_loop(0, max(1, probs[0].t_tile // CHUNK), step, 0)

    @pl.when(t_idx == pl.num_programs(1) - 1)
    def _():
        for p in probs:
            p.write_state()


def _recur_call(problems, layer, name):
    bsz, t_len, _ = problems[0][0].shape
    group = math.gcd(RECUR_GROUP, bsz)
    tt = min(RECUR_TT, t_len)
    has_init = problems[0][6] is not None
    seq = lambda w: pl.BlockSpec((group, tt, w), lambda b, t: (b, t, 0))
    in_specs, args, out_specs, out_shape, scratch, shapes = [], [], [], [], [], []
    for q, k, v, g, gate, nw, s0, heads, dk, dv in problems:
        kwp, kw, vw = q.shape[2], heads * dk, heads * dv
        st = pl.BlockSpec((group, kw, dv), lambda b, t: (b, 0, 0))
        in_specs += [seq(kwp), seq(kwp), seq(vw), seq(kwp), seq(vw), _layer_spec((1, vw), layer)]
        args += [q, k, v, g, gate, nw]
        if has_init:
            in_specs.append(st)
            args.append(s0)
        out_specs += [seq(vw), st]
        out_shape += [jax.ShapeDtypeStruct((bsz, t_len, vw), BF16), jax.ShapeDtypeStruct((bsz, kw, dv), F32)]
        scratch.append(pltpu.VMEM((group, vw, kwp), F32))
        shapes.append((heads, dk, dv))
    outs = pl.pallas_call(
        functools.partial(_recur_kernel, tuple(shapes), has_init),
        grid=(bsz // group, t_len // tt), in_specs=in_specs, out_specs=out_specs, out_shape=out_shape,
        scratch_shapes=scratch, name=name, compiler_params=_params("parallel", "arbitrary"),
    )(*args)
    return [tuple(outs[2 * i:2 * i + 2]) for i in range(len(problems))]


def _log_multiplicity(delta):
    delta = np.asarray(delta, np.int64)
    cnt = np.zeros(delta.shape, np.float64)
    for w, d in DILATED_PATTERNS:
        cnt += (delta >= 0) & (delta <= w) & (delta % d == 0)
    return np.where(cnt > 0, np.log2(np.maximum(cnt, 1.0)), NEG).astype(np.float32)


ATTN_UNITS_PER_GROUP = 8


def _prompt_attn_kernel(*refs):
    n_pat = len(DILATED_PATTERNS)
    qs, ks, vs = refs[0:3 * n_pat:3], refs[1:3 * n_pat:3], refs[2:3 * n_pat:3]
    mask_ref, o_ref = refs[3 * n_pat:3 * n_pat + 2]
    out_scr, *scr = refs[3 * n_pat + 2:]
    nd, md, ld = scr[0::3], scr[1::3], scr[2::3]
    t_len = o_ref.shape[1]
    dils = [d for _, d in DILATED_PATTERNS]
    lo = _iota((QBLK, LANES), 1) < HEAD_DIM

    units = []
    for di, d in enumerate(dils):
        for r in range(d):
            for pb in range(t_len // d // QBLK):
                q0 = pb * QBLK
                k0, klen = (q0, QBLK) if pb == 0 else (q0 - QBLK, 2 * QBLK)
                units.append((di, r, q0, k0, klen))

    eye = (_iota((QBLK, QBLK), 0) == _iota((QBLK, QBLK), 1)).astype(BF16)
    mask_t = mask_ref[...]

    def run(group):
        chains = [(u, e) for u in group for e in (0, 1)]
        scores = []
        for (di, r, q0, k0, klen), e in chains:
            q = qs[di][0, r, q0:q0 + QBLK, :]
            q = jnp.where(lo, q, jnp.zeros_like(q)) if e == 0 else jnp.where(lo, jnp.zeros_like(q), q)
            k_aug = jnp.concatenate([ks[di][0, r, k0:k0 + klen, :], mask_t[2 * QBLK - klen:2 * QBLK]], axis=1)
            scores.append(_dot_nt(jnp.concatenate([q, eye], axis=1), k_aug))
        for i, (di, r, q0, k0, klen) in enumerate(group):
            s = scores[2 * i:2 * i + 2]
            m = [jnp.max(x, axis=1, keepdims=True) for x in s]
            p = [jnp.exp2(x - mx) for x, mx in zip(s, m)]
            l = [jnp.sum(x, axis=1, keepdims=True) for x in p]
            num = [_dot(x.astype(BF16), vs[di][0, r, k0:k0 + klen, :]) for x in p]
            n_u = jnp.where(lo, num[0], num[1])
            m_u = jnp.where(lo, m[0], m[1])
            l_u = jnp.where(lo, l[0], l[1])
            if di < n_pat - 1:
                d = dils[di]
                f = d_max // d
                for dst, val in ((nd[di], n_u), (md[di], m_u), (ld[di], l_u)):
                    by_res = jnp.swapaxes(val.reshape(QBLK // f, f, LANES), 0, 1)
                    for c in range(f):
                        dst[d * c + r, q0 // f:(q0 + QBLK) // f, :] = by_res[c]
                continue
            ms = [mr[r] for mr in md] + [m_u]
            top = functools.reduce(jnp.maximum, ms)
            w = [jnp.exp2(x - top) for x in ms]
            acc = sum(wx * nx for wx, nx in zip(w, [nr[r] for nr in nd] + [n_u]))
            den = sum(wx * lx for wx, lx in zip(w, [lr[r] for lr in ld] + [l_u]))
            out_scr[r] = acc * (1.0 / den)

    d_max = dils[-1]
    for i in range(0, len(units), ATTN_UNITS_PER_GROUP):
        run(units[i:i + ATTN_UNITS_PER_GROUP])
    rows_per = t_len // d_max // 4
    for blk in range(4):
        x = out_scr[:, blk * rows_per:(blk + 1) * rows_per, :]
        o_ref[0, blk * rows_per * d_max:(blk + 1) * rows_per * d_max, :] = (
            jnp.swapaxes(x, 0, 1).reshape(rows_per * d_max, LANES).astype(o_ref.dtype))


def _prompt_attn_call(streams, t_len):
    bsz = streams[0].shape[0]
    dils = [d for _, d in DILATED_PATTERNS]
    assert all(w // d == QBLK for w, d in DILATED_PATTERNS) and t_len == QBLK * dils[-1] == QBLK * max(dils)
    a = np.arange(QBLK)[:, None]
    c = np.arange(2 * QBLK)[None, :]
    mask_t = jnp.asarray(np.where((c >= a) & (c <= a + QBLK), 0.0, NEG).astype(np.float32).T, dtype=BF16)
    n_grp = ATT_W // LANES
    in_specs, args = [], []
    for s_d in streams:
        d, n = s_d.shape[1], s_d.shape[2]
        for part in range(3):
            in_specs.append(pl.BlockSpec((1, d, n, LANES), lambda b, j, part=part: (b, 0, 0, part * n_grp + j)))
            args.append(s_d)
    by_residue = pltpu.VMEM((dils[-1], t_len // dils[-1], LANES), F32)
    scratch = [by_residue] * (1 + 3 * (len(dils) - 1))
    return pl.pallas_call(
        _prompt_attn_kernel, grid=(bsz, n_grp),
        in_specs=in_specs + [_const_spec((2 * QBLK, QBLK))],
        out_specs=pl.BlockSpec((1, t_len, LANES), lambda b, j: (b, 0, j)),
        out_shape=jax.ShapeDtypeStruct((bsz, t_len, ATT_W), BF16), name="prompt_attn",
        scratch_shapes=scratch,
        compiler_params=_params("parallel", "parallel"),
    )(*args, mask_t)


def _sample_attn_kernel(with_prev, *refs):
    if with_prev:
        q_ref, kn_ref, vn_ref, ck_ref, cv_ref, bias_ref, _, _, o_ref, ok_ref, ov_ref, kb_scr, vb_scr = refs
    else:
        q_ref, kn_ref, vn_ref, ck_ref, cv_ref, bias_ref, o_ref, ok_ref, ov_ref, kb_scr, vb_scr = refs
    win = ck_ref.shape[3]
    t_new = kn_ref.shape[1]
    tail = _iota((LANES, LANES), 1) >= LANES - t_new
    for c_ref, n_ref, out_ref, scr in ((ck_ref, kn_ref, ok_ref, kb_scr), (cv_ref, vn_ref, ov_ref, vb_scr)):
        new_t = jnp.concatenate([n_ref[0], jnp.zeros((LANES - t_new, ATT_W), F32)], axis=0).T
        for rb in range(ATT_W // LANES):
            rows = slice(rb * LANES, (rb + 1) * LANES)
            old = c_ref[0, 0, rows, :]
            shifted = pltpu.roll(old, win - t_new, 1)
            out_ref[0, 0, rows, 0:win - LANES] = shifted[:, 0:win - LANES]
            out_ref[0, 0, rows, win - LANES:win] = jnp.where(
                tail, pltpu.roll(new_t[rows], LANES - t_new, 1), shifted[:, win - LANES:win])
            scr[rows, 0:win] = old.astype(BF16)
            scr[rows, win:win + LANES] = new_t[rows].astype(BF16)
    q = q_ref[0].astype(F32)
    rows = ATT_HEADS * t_new
    own = (_iota((rows, ATT_W), 0) // t_new) == (_iota((rows, ATT_W), 1) // HEAD_DIM)
    q_stack = jnp.where(own, jnp.concatenate([q] * ATT_HEADS, axis=0), 0.0).astype(BF16)
    s = _dot(q_stack, kb_scr[...]) + bias_ref[...]
    m = jnp.max(s, axis=1, keepdims=True)
    p = jnp.exp2(s - m)
    l = jnp.sum(p, axis=1, keepdims=True)
    o_all = jnp.where(own, _dot_nt(p.astype(BF16), vb_scr[...]) * (1.0 / l), 0.0)
    o = o_all[0:t_new]
    for h in range(1, ATT_HEADS):
        o = o + o_all[h * t_new:(h + 1) * t_new]
    o_ref[0] = o.astype(o_ref.dtype)


def _sample_attn_call(q, k_new, v_new, cache_kt, cache_vt, prev, layer):
    depth, bsz, _, win = cache_kt.shape
    t_new = q.shape[1]
    rows = ATT_HEADS * t_new
    tq = np.arange(rows)[:, None] % t_new
    n = np.arange(win + LANES)[None, :]
    bias = np.where(n < win + t_new, _log_multiplicity(win + tq - n), NEG).astype(np.float32)
    new = pl.BlockSpec((1, t_new, ATT_W), lambda b: (b, 0, 0))
    cache = pl.BlockSpec((1, 1, ATT_W, win), lambda b: (layer, b, 0, 0))
    in_specs = [new, new, new, cache, cache, _const_spec((rows, win + LANES))]
    args = [q, k_new, v_new, cache_kt, cache_vt, jnp.asarray(bias)]
    aliases = {}
    if prev is not None:
        in_specs += [pl.BlockSpec(memory_space=pl.ANY)] * 2
        args += list(prev)
        aliases = {6: 1, 7: 2}
    sd = jax.ShapeDtypeStruct
    return pl.pallas_call(
        functools.partial(_sample_attn_kernel, prev is not None), grid=(bsz,),
        in_specs=in_specs, out_specs=[new, cache, cache],
        out_shape=[sd((bsz, t_new, ATT_W), BF16), sd(cache_kt.shape, F32), sd(cache_vt.shape, F32)],
        scratch_shapes=[pltpu.VMEM((ATT_W, win + LANES), BF16)] * 2,
        input_output_aliases=aliases, name="sample_attn", compiler_params=_params("parallel"),
    )(*args)


FF_CHUNK = 1024


def _mlp_kernel(h_ref, oh_ref, oa_ref, og_ref, p_ref, an_ref, wo_ref, npm_ref, npre_ref, wup_ref, wdn_ref,
                npost_ref, wpg_ref, wple_ref, out_ref):
    half = h_ref.shape[0] // 2
    halves = [slice(0, half), slice(half, 2 * half)]
    oa = [(_rms(oa_ref[s, :].astype(F32)) * an_ref[...]).astype(BF16) for s in halves]
    mix = [_dot(jnp.concatenate([oh_ref[s, :], oa[i], og_ref[s, :]], axis=1), wo_ref[...])
           for i, s in enumerate(halves)]
    ple = [_dot(p_ref[s, :].astype(BF16), wple_ref[...]) for s in halves]
    h = [h_ref[s, :] + _rms(mix[i]) * npm_ref[...] for i, s in enumerate(halves)]
    xn = [(_rms(x) * npre_ref[...]).astype(BF16) for x in h]
    acc = [None, None]
    for c in range(D_FF // FF_CHUNK):
        cols = slice(c * FF_CHUNK, (c + 1) * FF_CHUNK)
        u = [jnp.maximum(_dot(x, wup_ref[:, cols]), 0.0) for x in xn]
        part = [_dot((x * x).astype(BF16), wdn_ref[cols, :]) for x in u]
        acc = [p if a is None else a + p for a, p in zip(acc, part)]
    h = [x + _rms(a) * npost_ref[...] for x, a in zip(h, acc)]
    gate = [_sigmoid(_dot(x.astype(BF16), wpg_ref[...])) for x in h]
    for i, s in enumerate(halves):
        out_ref[s, :] = h[i] + gate[i] * ple[i]


def _mlp_call(h2d, oh, oa, og, p3d, layer, an, wo, npm, npre, wup, wdn, npost, wpg, wple, tm):
    n = h2d.shape[0]
    row = lambda w: pl.BlockSpec((tm, w), lambda i: (i, 0))
    lspec = lambda *shape: _layer_spec(shape, layer)
    vec = lspec(1, D_MODEL)
    return pl.pallas_call(
        _mlp_kernel, grid=(n // tm,),
        in_specs=[row(D_MODEL), row(HG_W), row(ATT_W), row(GLA_VW),
                  pl.BlockSpec((None, tm, D_PLE), lambda i: (layer, i, 0)), lspec(1, ATT_W),
                  lspec(D_MODEL, D_MODEL), vec, vec, lspec(D_MODEL, D_FF), lspec(D_FF, D_MODEL), vec,
                  lspec(D_MODEL, D_MODEL), lspec(D_PLE, D_MODEL)],
        out_specs=row(D_MODEL), out_shape=jax.ShapeDtypeStruct((n, D_MODEL), F32), name="mlp",
        compiler_params=_params("parallel"),
    )(h2d, oh, oa, og, p3d, an, wo, npm, npre, wup, wdn, npost, wpg, wple)


def _pack_w_in(w_in):
    splits = np.cumsum([0, HG_W, HG_W, HG_W, HG_W, ATT_W, ATT_W, ATT_W, GLA_KW, GLA_KW, GLA_VW,
                        GLA_GATE_RANK, GLA_VW])
    col = lambda i: w_in[..., splits[i]:splits[i + 1]]
    padc = lambda a, w: jnp.pad(a, ((0, 0), (0, 0), (0, w - a.shape[-1])))
    parts = [col(0), col(1), col(2), col(3), col(4), col(5), col(6),
             padc(col(7), GLA_KP), padc(col(8), GLA_KP), col(9), col(11), padc(col(10), LANES)]
    return jnp.concatenate(parts, axis=-1).astype(BF16)


def _rope_tables(pos):
    half = ROT_DIM // 2
    inv = jnp.exp(-math.log(ROPE_THETA) * jnp.arange(half, dtype=F32) * (2.0 / ROT_DIM))
    ang = pos[:, None] * inv[None, :]
    cos, sin = jnp.cos(ang), jnp.sin(ang)
    d = np.arange(LANES) % HEAD_DIM
    first, second = d < half, (d >= half) & (d < ROT_DIM)
    idx = np.where(second, d - half, np.where(first, d, 0))
    cos_t = jnp.where(first | second, cos[:, idx], 1.0)
    sin_a = jnp.where(first, -sin[:, idx], 0.0)
    sin_b = jnp.where(second, sin[:, idx], 0.0)
    return cos_t, sin_a, sin_b


def kernel(x_prompt, x_sample, state_hgrn, state_gla, cache_k, cache_v, p_prompt, p_sample, norm_pre_mix, w_in, hgrn_lb, hgrn_norm, attn_norm, gla_w_gate2, gla_b_gate, gla_norm, w_out, norm_post_mix, norm_pre_mlp, w_up, w_down, norm_post_mlp, w_ple_gate, w_ple):
    depth = w_in.shape[0]
    bp, tp, _ = x_prompt.shape
    bs, ts, _ = x_sample.shape
    win = cache_k.shape[2]

    lb_cum = jnp.cumsum(jax.nn.softmax(hgrn_lb.astype(F32), axis=0), axis=0)
    lower_bounds = lb_cum - lb_cum[0:1]

    tm_p = min(TOKEN_TILE, tp)
    tm_s = bs * ts
    rope_p = _rope_tables(jnp.arange(tp, dtype=F32))
    rope_s = tuple(jnp.tile(t, (bs, 1)) for t in _rope_tables(jnp.arange(ts, dtype=F32) + PAST_LEN))
    to_t = lambda a: jnp.transpose(a, (0, 1, 3, 4, 2)).reshape(depth, a.shape[1], ATT_W, a.shape[2])
    from_t = lambda a: jnp.transpose(a.reshape(depth, a.shape[1], ATT_HEADS, HEAD_DIM, a.shape[3]), (0, 1, 4, 2, 3))
    ck_t, cv_t = to_t(cache_k), to_t(cache_v)
    pp3 = p_prompt.reshape(depth, bp * tp, D_PLE)
    ps3 = p_sample.reshape(depth, bs * ts, D_PLE)

    hp = x_prompt.reshape(bp * tp, D_MODEL)
    hs = x_sample.reshape(bs * ts, D_MODEL)
    kv_p = None
    kv_s = None
    hg_p, gl_p, hg_s, gl_s = [], [], [], []
    row = lambda a: a.reshape(depth, 1, -1)
    w_all = _pack_w_in(w_in)
    wg2 = jnp.pad(gla_w_gate2, ((0, 0), (0, LANES - GLA_GATE_RANK), (0, GLA_KP - GLA_KW))).astype(BF16)
    bg = row(jnp.pad(gla_b_gate, ((0, 0), (0, GLA_KP - GLA_KW))))
    proj_w = (row(norm_pre_mix), w_all, row(lower_bounds))
    mlp_w = (row(attn_norm), w_out.astype(BF16), row(norm_post_mix), row(norm_pre_mlp), w_up.astype(BF16),
             w_down.astype(BF16), row(norm_post_mlp), w_ple_gate.astype(BF16), w_ple.astype(BF16))
    hgrn_nw, gla_nw = row(hgrn_norm), row(gla_norm)
    for i in range(depth):
        (hq, hk, hv, hlf, hgt, gq, gk, gv, gla, ggt, *att_streams, k_t, v_t) = _proj_call(
            hp, *proj_w, rope_p, wg2, bg, kv_p, i, i, depth, tm_p, tp)
        kv_p = (k_t, v_t)
        sh = lambda a: a.reshape(bp, tp, a.shape[-1])
        (oh, s_h), (og, s_g) = _recur_call(
            [(sh(hq), sh(hk), sh(hv), sh(hlf), sh(hgt), hgrn_nw, None, HG_HEADS, HG_DK, HG_DV),
             (sh(gq), sh(gk), sh(gv), sh(gla), sh(ggt), gla_nw, None, GLA_HEADS, GLA_DK, GLA_DV)], i, "recur")
        oa = _prompt_attn_call(att_streams, tp)
        fl = lambda a: a.reshape(bp * tp, a.shape[-1])
        hp = _mlp_call(hp, fl(oh), fl(oa), fl(og), pp3, i, *mlp_w, tm_p)
        hg_p.append(s_h.reshape(bp, HG_HEADS, HG_DK, HG_DV))
        gl_p.append(s_g.reshape(bp, GLA_HEADS, GLA_DK, GLA_DV))

        (hq, hk, hv, hlf, hgt, gq, gk, gv, gla, ggt, aq, k_new, v_new) = _proj_call(
            hs, *proj_w, rope_s, wg2, bg, None, i, 0, 1, tm_s, None)
        pt = lambda a: a.reshape(bs, ts, a.shape[-1])
        (oh, s_h), (og, s_g) = _recur_call(
            [(pt(hq), pt(hk), pt(hv), pt(hlf), pt(hgt), hgrn_nw, state_hgrn[i].reshape(bs, HG_W, HG_DV),
              HG_HEADS, HG_DK, HG_DV),
             (pt(gq), pt(gk), pt(gv), pt(gla), pt(ggt), gla_nw, state_gla[i].reshape(bs, GLA_KW, GLA_DV),
              GLA_HEADS, GLA_DK, GLA_DV)], i, "recur_s")
        s3 = lambda a: a.reshape(bs, ts, ATT_W)
        oa, ck_new, cv_new = _sample_attn_call(s3(aq), s3(k_new), s3(v_new), ck_t, cv_t, kv_s, i)
        kv_s = (ck_new, cv_new)
        ut = lambda a: a.reshape(bs * ts, a.shape[-1])
        hs = _mlp_call(hs, ut(oh), oa.reshape(bs * ts, ATT_W), ut(og), ps3, i, *mlp_w, tm_s)
        hg_s.append(s_h.reshape(bs, HG_HEADS, HG_DK, HG_DV))
        gl_s.append(s_g.reshape(bs, GLA_HEADS, GLA_DK, GLA_DV))

    return (hp.reshape(bp, tp, D_MODEL), hs.reshape(bs, ts, D_MODEL),
            jnp.stack(hg_p), jnp.stack(gl_p), from_t(kv_p[0]), from_t(kv_p[1]),
            jnp.stack(hg_s), jnp.stack(gl_s), from_t(kv_s[0]), from_t(kv_s[1]))
```

```python
import functools
import itertools
import math

import jax
import jax.numpy as jnp
import numpy as np
from jax import lax
from jax.experimental import pallas as pl
from jax.experimental.pallas import tpu as pltpu

F32 = jnp.float32
BF16 = jnp.bfloat16

D_MODEL = 1024
HEAD_DIM = 64
HG_HEADS, HG_DK, HG_DV = 4, 64, 64
ATT_HEADS = 6
GLA_HEADS, GLA_DK, GLA_DV = 6, 32, 64
GLA_GATE_RANK = 16
GLA_TAU = 16.0
D_FF = 4 * D_MODEL
D_PLE = 256
ROPE_THETA = 500000.0
ROT_DIM = HEAD_DIM // 4
DILATED_PATTERNS = ((128, 1), (512, 4), (2048, 16))
PAST_LEN = 16384
EPS = 1e-6

LANES = 128
HG_W = HG_HEADS * HG_DK
ATT_W = ATT_HEADS * HEAD_DIM
GLA_KW = GLA_HEADS * GLA_DK
GLA_KP = 256
GLA_VW = GLA_HEADS * GLA_DV

CHUNK = 64
HALF = CHUNK // 2
TOKEN_TILE = 512
RECUR_GROUP = 8
RECUR_TT = 256
QBLK = 128
NEG = -1e30
LOG2_E = math.log2(math.e)
EXP2_CLAMP = 115.0
Q_SCALE = HEAD_DIM ** -0.5 * LOG2_E
VMEM_LIMIT = 56 * 1024 * 1024

C_HG = 0
C_AT = C_HG + 4 * HG_W
C_GL = C_AT + 3 * ATT_W
GL_Q, GL_K, GL_V, GL_R, GL_LR = 0, 256, 512, 896, 1280
GL_COLS = 1408
W_COLS = C_GL + GL_COLS


def _rms(x):
    return x * lax.rsqrt(jnp.mean(x * x, axis=-1, keepdims=True) + EPS)


def _sigmoid(x):
    return 1.0 / (1.0 + jnp.exp(-x))


def _dot(a, b):
    return jnp.dot(a, b, preferred_element_type=F32)


def _dot_nt(a, b):
    return lax.dot_general(a, b, (((1,), (1,)), ((), ())), preferred_element_type=F32)


def _dot_tn(a, b):
    return lax.dot_general(a, b, (((0,), (0,)), ((), ())), preferred_element_type=F32)


def _iota(shape, d):
    return lax.broadcasted_iota(jnp.int32, shape, d)


def _const_spec(shape):
    nd = len(shape)
    return pl.BlockSpec(shape, lambda *_: (0,) * nd, pipeline_mode=pl.Buffered(1))


def _layer_spec(shape, layer):
    nd = len(shape)
    return pl.BlockSpec((None,) + tuple(shape), lambda *_: (layer,) + (0,) * nd, pipeline_mode=pl.Buffered(1))


def _params(*semantics):
    return pltpu.CompilerParams(dimension_semantics=semantics, vmem_limit_bytes=VMEM_LIMIT)


def _proj_kernel(streams, x_ref, gpre_ref, w_ref, lb_ref, cos_ref, sa_ref, sb_ref, wg2_ref, bg_ref,
                 hq_o, hk_o, hv_o, hlf_o, hgt_o, gq_o, gk_o, gv_o, gla_o, ggt_o, *att_refs):
    xn = (_rms(x_ref[...]) * gpre_ref[...]).astype(BF16)

    y_at = _dot(xn, w_ref[:, C_AT:C_AT + 3 * ATT_W])
    y_hg = _dot(xn, w_ref[:, C_HG:C_HG + 4 * HG_W])
    y_gl = _dot(xn, w_ref[:, C_GL:C_GL + GL_COLS])

    y = y_at
    cos_t, sin_a, sin_b = cos_ref[...], sa_ref[...], sb_ref[...]

    def rope(v):
        return v * cos_t + pltpu.roll(v, LANES - ROT_DIM // 2, 1) * sin_a + pltpu.roll(v, ROT_DIM // 2, 1) * sin_b

    if streams:
        *stream_os, ak_o, av_o = att_refs
    else:
        aq_o, ak_o, av_o = att_refs
    for j in range(ATT_W // LANES):
        sl = slice(j * LANES, (j + 1) * LANES)
        q_rot = rope(y[:, sl]) * Q_SCALE
        k_rot = rope(y[:, ATT_W + j * LANES:ATT_W + (j + 1) * LANES])
        v_grp = y[:, 2 * ATT_W + j * LANES:2 * ATT_W + (j + 1) * LANES]
        if streams:
            ak_o[0, 0, sl, :] = k_rot.T
            av_o[0, 0, sl, :] = v_grp.T
            for s_o in stream_os:
                d, n = s_o.shape[1], s_o.shape[2]
                for part, x in enumerate((q_rot, k_rot, v_grp)):
                    x = jnp.swapaxes(x.reshape(n, d, LANES), 0, 1) if d > 1 else x.reshape(1, n, LANES)
                    lanes = slice(part * ATT_W + j * LANES, part * ATT_W + (j + 1) * LANES)
                    s_o[0, :, :, lanes] = x.astype(BF16)
        else:
            aq_o[:, sl] = q_rot
            ak_o[:, sl] = k_rot
            av_o[:, sl] = v_grp

    y = y_hg
    lb = lb_ref[...]
    f = lb + (1.0 - lb) * _sigmoid(y[:, HG_W:2 * HG_W])
    hg = y[:, 3 * HG_W:4 * HG_W]
    hq_o[...] = y[:, 0:HG_W]
    hk_o[...] = 1.0 - f
    hv_o[...] = y[:, 2 * HG_W:3 * HG_W].astype(BF16)
    hlf_o[...] = jnp.log2(f)
    hgt_o[...] = (hg * _sigmoid(hg)).astype(BF16)

    y = y_gl
    gr = y[:, GL_R:GL_R + GLA_VW]
    z = _dot(y[:, GL_LR:GL_LR + LANES].astype(BF16), wg2_ref[...]) + bg_ref[...]
    log_a = (jnp.minimum(z, 0.0) - jnp.log(1.0 + jnp.exp(-jnp.abs(z)))) * (LOG2_E / GLA_TAU)
    gq_o[...] = y[:, GL_Q:GL_Q + GLA_KP] * (GLA_DK ** -0.5)
    gk_o[...] = y[:, GL_K:GL_K + GLA_KP]
    gv_o[...] = y[:, GL_V:GL_V + GLA_VW].astype(BF16)
    gla_o[...] = log_a
    ggt_o[...] = (gr * _sigmoid(gr)).astype(BF16)


def _proj_call(x2d, gpre, w_all, lb, rope_tabs, wg2, bg, kv_prev, w_layer, layer, depth, tm, seq_len):
    n = x2d.shape[0]
    cos_t, sin_a, sin_b = rope_tabs
    tab_blocks = cos_t.shape[0] // tm
    row = lambda w: pl.BlockSpec((tm, w), lambda i: (i, 0))
    tab = pl.BlockSpec((tm, LANES), lambda i: (i % tab_blocks, 0))
    sd = jax.ShapeDtypeStruct
    lspec = lambda *shape: _layer_spec(shape, w_layer)
    in_specs = [row(D_MODEL), lspec(1, D_MODEL), lspec(D_MODEL, W_COLS), lspec(1, HG_W), tab, tab, tab,
                lspec(LANES, GLA_KP), lspec(1, GLA_KP)]
    args = [x2d, gpre, w_all, lb, cos_t, sin_a, sin_b, wg2, bg]
    out_shape = [sd((n, HG_W), F32), sd((n, HG_W), F32), sd((n, HG_W), BF16), sd((n, HG_W), F32),
                 sd((n, HG_W), BF16), sd((n, GLA_KP), F32), sd((n, GLA_KP), F32),
                 sd((n, GLA_VW), BF16), sd((n, GLA_KP), F32), sd((n, GLA_VW), BF16)]
    out_specs = [row(HG_W)] * 5 + [row(GLA_KP), row(GLA_KP), row(GLA_VW), row(GLA_KP), row(GLA_VW)]
    aliases, scratch = {}, []
    if seq_len is not None:
        nt, bsz = seq_len // tm, n // seq_len
        for _, d in DILATED_PATTERNS:
            out_shape.append(sd((bsz, d, seq_len // d, 3 * ATT_W), BF16))
            out_specs.append(pl.BlockSpec((1, d, tm // d, 3 * ATT_W), lambda i: (i // nt, 0, i % nt, 0)))
        out_shape += [sd((depth, bsz, ATT_W, seq_len), F32)] * 2
        out_specs += [pl.BlockSpec((1, 1, ATT_W, tm), lambda i: (layer, i // nt, 0, i % nt))] * 2
        if kv_prev is not None:
            in_specs += [pl.BlockSpec(memory_space=pl.ANY)] * 2
            args += list(kv_prev)
            aliases = {len(args) - 2: len(out_shape) - 2, len(args) - 1: len(out_shape) - 1}
    else:
        out_shape += [sd((n, ATT_W), F32)] * 3
        out_specs += [row(ATT_W)] * 3
    n_in = len(args)

    def body(*refs):
        _proj_kernel(seq_len is not None, *refs[:9], *refs[n_in:])

    return pl.pallas_call(
        body, grid=(n // tm,), in_specs=in_specs, out_specs=out_specs, out_shape=out_shape,
        scratch_shapes=scratch, input_output_aliases=aliases, name="proj", compiler_params=_params("parallel"),
    )(*args)


class _Recurrence:
    def __init__(self, heads, dk, dv, in_refs, out_refs, s_scr):
        self.heads, self.dk, self.dv = heads, dk, dv
        self.q_ref, self.k_ref, self.v_ref, self.g_ref, self.gate_ref, self.nw_ref = in_refs[:6]
        self.s0_ref = in_refs[6] if len(in_refs) > 6 else None
        self.o_ref, self.s_out_ref = out_refs
        self.s_scr = s_scr
        self.group, self.t_tile, self.kwp = self.q_ref.shape
        self.kw, self.vw = heads * dk, heads * dv
        per_k = LANES // dk
        self.per_v = LANES // dv
        kwp, vw = self.kwp, self.vw
        self.tri = (_iota((CHUNK, CHUNK), 0) >= _iota((CHUNK, CHUNK), 1)).astype(BF16)
        self.causal = [(_iota((per_k * HALF, CHUNK), 0) % HALF) + h * HALF >= _iota((per_k * HALF, CHUNK), 1)
                       for h in range(2)]
        self.own_head = ((_iota((per_k * HALF, LANES), 0) // HALF)
                         == (_iota((per_k * HALF, LANES), 1) // dk)).astype(BF16)
        self.bd_mask = (_iota((vw, kwp), 0) // dv) == (_iota((vw, kwp), 1) // dk)
        self.pool = jnp.where((_iota((vw, vw), 0) // dv) == (_iota((vw, vw), 1) // dv), 1.0 / dv, 0.0).astype(BF16)
        self.v_head = _iota((CHUNK, LANES), 1) // dv
        self.k_groups = [(kg, min(per_k, heads - kg * per_k)) for kg in range(kwp // LANES) if heads > kg * per_k]
        self.short = self.t_tile < CHUNK

    def init_state(self):
        for g in range(self.group):
            if self.s0_ref is None:
                self.s_scr[g] = jnp.zeros((self.vw, self.kwp), F32)
                continue
            s_nat = self.s0_ref[g]
            if self.kwp > self.kw:
                s_nat = jnp.concatenate([s_nat, jnp.zeros((self.kwp - self.kw, self.dv), F32)], axis=0)
            tiled = jnp.concatenate([s_nat] * self.heads, axis=1)
            self.s_scr[g] = jnp.where(self.bd_mask, tiled.T, 0.0)

    def write_state(self):
        for g in range(self.group):
            s_bd = self.s_scr[g].T
            s_nat = s_bd[:, 0:self.dv]
            for h in range(1, self.heads):
                s_nat = s_nat + s_bd[:, h * self.dv:(h + 1) * self.dv]
            self.s_out_ref[g] = s_nat[0:self.kw]

    def _load(self, ref, g, rows):
        if not self.short:
            return ref[g, rows, :]
        x = ref[g].astype(F32)
        return jnp.concatenate([x, jnp.zeros((CHUNK - self.t_tile, x.shape[1]), F32)], axis=0)

    @staticmethod
    def _split_dot(a, x, terms):
        acc = None
        for _ in range(terms):
            hi = x.astype(BF16)
            part = _dot(a, hi)
            acc = part if acc is None else acc + part
            x = x - hi.astype(F32)
        return acc

    def chunk_stages(self, rows):
        seqs = range(self.group)
        load, s_scr = self._load, self.s_scr
        b = [self._split_dot(self.tri, load(self.g_ref, g, rows), 2) for g in seqs]
        yield
        q_in, k_in, q_st, k_st, dec = [], [], [], [], []
        for g in seqs:
            q = load(self.q_ref, g, rows).astype(F32)
            k = load(self.k_ref, g, rows).astype(F32)
            b_half, b_last = b[g][HALF - 1:HALF, :], b[g][CHUNK - 1:CHUNK, :]
            refs_g = (0.5 * b_half, 0.5 * (b_half + b_last))
            q_in.append([(q[h * HALF:(h + 1) * HALF] * jnp.exp2(jnp.minimum(
                b[g][h * HALF:(h + 1) * HALF] - refs_g[h], EXP2_CLAMP))).astype(BF16) for h in range(2)])
            k_in.append([(k * jnp.exp2(jnp.minimum(r - b[g], EXP2_CLAMP))).astype(BF16) for r in refs_g])
            q_st.append((q * jnp.exp2(b[g])).astype(BF16))
            k_st.append((k * jnp.exp2(b_last - b[g])).astype(BF16))
            dec.append(jnp.exp2(b_last))
        yield
        a = []
        for g in seqs:
            per_group = []
            for kg, nh in self.k_groups:
                sl = slice(kg * LANES, (kg + 1) * LANES)
                halves = []
                for h in range(2):
                    stack = jnp.concatenate([q_in[g][h][:, sl]] * nh, axis=0) * self.own_head[0:nh * HALF]
                    halves.append(_dot_nt(stack, k_in[g][h][:, sl]))
                per_group.append(halves)
            a.append(per_group)
        yield
        s_old = [s_scr[g] for g in seqs]
        o_inter = [_dot_nt(q_st[g], s_old[g].astype(BF16)) for g in seqs]
        v = [load(self.v_ref, g, rows).astype(BF16) for g in seqs]
        u = [_dot_tn(v[g], k_st[g]) for g in seqs]
        yield
        for g in seqs:
            s_scr[g] = s_old[g] * dec[g] + jnp.where(self.bd_mask, u[g], 0.0)
        o = []
        for g in seqs:
            a_heads = []
            for (kg, nh), ag in zip(self.k_groups, a[g]):
                am = [jnp.where(self.causal[h][0:nh * HALF], ag[h], 0.0).astype(BF16) for h in range(2)]
                a_heads += [jnp.concatenate([am[h][e * HALF:(e + 1) * HALF] for h in range(2)], axis=0)
                            for e in range(nh)]
            pieces = []
            for j in range(self.vw // LANES):
                v_grp = v[g][:, j * LANES:(j + 1) * LANES]
                acc = None
                for e in range(self.per_v):
                    oe = _dot(a_heads[j * self.per_v + e], v_grp)
                    acc = oe if acc is None else jnp.where(self.v_head == e, oe, acc)
                pieces.append(acc)
            o.append(o_inter[g] + jnp.concatenate(pieces, axis=1))
        yield
        ms_all = _dot(jnp.concatenate([(o[g] * o[g]).astype(BF16) for g in seqs], axis=0), self.pool)
        yield
        nw = self.nw_ref[...]
        for g in seqs:
            gate = load(self.gate_ref, g, rows).astype(F32)
            out = (o[g] * lax.rsqrt(ms_all[g * CHUNK:(g + 1) * CHUNK] + EPS) * nw * gate).astype(self.o_ref.dtype)
            if self.short:
                self.o_ref[g] = out[0:self.t_tile]
            else:
                self.o_ref[g, rows, :] = out


def _recur_kernel(shapes, has_init, *refs):
    n_in = 7 if has_init else 6
    n = len(shapes)
    probs = [_Recurrence(*shp, refs[i * n_in:(i + 1) * n_in], refs[n * n_in + 2 * i:n * n_in + 2 * i + 2],
                         refs[n * (n_in + 2) + i]) for i, shp in enumerate(shapes)]
    t_idx = pl.program_id(1)

    @pl.when(t_idx == 0)
    def _():
        for p in probs:
            p.init_state()

    def step(c, carry):
        rows = pl.ds(pl.multiple_of(c * CHUNK, CHUNK), CHUNK)
        for _ in itertools.zip_longest(*[p.chunk_stages(rows) for p in probs]):
            pass
        return carry

    lax.fori_loop(0, max(1, probs[0].t_tile // CHUNK), step, 0)

    @pl.when(t_idx == pl.num_programs(1) - 1)
    def _():
        for p in probs:
            p.write_state()


def _recur_call(problems, layer, name):
    bsz, t_len, _ = problems[0][0].shape
    group = math.gcd(RECUR_GROUP * (2 if t_len < CHUNK else 1), bsz)
    tt = min(RECUR_TT, t_len)
    has_init = problems[0][6] is not None
    seq = lambda w: pl.BlockSpec((group, tt, w), lambda b, t: (b, t, 0))
    in_specs, args, out_specs, out_shape, scratch, shapes = [], [], [], [], [], []
    for q, k, v, g, gate, nw, s0, heads, dk, dv in problems:
        kwp, kw, vw = q.shape[2], heads * dk, heads * dv
        st = pl.BlockSpec((group, kw, dv), lambda b, t: (b, 0, 0))
        in_specs += [seq(kwp), seq(kwp), seq(vw), seq(kwp), seq(vw), _layer_spec((1, vw), layer)]
        args += [q, k, v, g, gate, nw]
        if has_init:
            in_specs.append(st)
            args.append(s0)
        out_specs += [seq(vw), st]
        out_shape += [jax.ShapeDtypeStruct((bsz, t_len, vw), BF16), jax.ShapeDtypeStruct((bsz, kw, dv), F32)]
        scratch.append(pltpu.VMEM((group, vw, kwp), F32))
        shapes.append((heads, dk, dv))
    outs = pl.pallas_call(
        functools.partial(_recur_kernel, tuple(shapes), has_init),
        grid=(bsz // group, t_len // tt), in_specs=in_specs, out_specs=out_specs, out_shape=out_shape,
        scratch_shapes=scratch, name=name, compiler_params=_params("parallel", "arbitrary"),
    )(*args)
    return [tuple(outs[2 * i:2 * i + 2]) for i in range(len(problems))]


def _log_multiplicity(delta):
    delta = np.asarray(delta, np.int64)
    cnt = np.zeros(delta.shape, np.float64)
    for w, d in DILATED_PATTERNS:
        cnt += (delta >= 0) & (delta <= w) & (delta % d == 0)
    return np.where(cnt > 0, np.log2(np.maximum(cnt, 1.0)), NEG).astype(np.float32)


ATTN_UNITS_PER_GROUP = 8


def _prompt_attn_kernel(*refs):
    n_pat = len(DILATED_PATTERNS)
    qs, ks, vs = refs[0:3 * n_pat:3], refs[1:3 * n_pat:3], refs[2:3 * n_pat:3]
    mask_ref, o_ref = refs[3 * n_pat:3 * n_pat + 2]
    out_scr, *scr = refs[3 * n_pat + 2:]
    nd, md, ld = scr[0::3], scr[1::3], scr[2::3]
    t_len = o_ref.shape[1]
    dils = [d for _, d in DILATED_PATTERNS]
    lo = _iota((QBLK, LANES), 1) < HEAD_DIM

    units = []
    for di, d in enumerate(dils):
        for r in range(d):
            for pb in range(t_len // d // QBLK):
                q0 = pb * QBLK
                k0, klen = (q0, QBLK) if pb == 0 else (q0 - QBLK, 2 * QBLK)
                units.append((di, r, q0, k0, klen))

    eye = (_iota((QBLK, QBLK), 0) == _iota((QBLK, QBLK), 1)).astype(BF16)
    mask_t = mask_ref[...]

    def run(group):
        chains = [(u, e) for u in group for e in (0, 1)]
        scores = []
        for (di, r, q0, k0, klen), e in chains:
            q = qs[di][0, r, q0:q0 + QBLK, :]
            q = jnp.where(lo, q, jnp.zeros_like(q)) if e == 0 else jnp.where(lo, jnp.zeros_like(q), q)
            k_aug = jnp.concatenate([ks[di][0, r, k0:k0 + klen, :], mask_t[2 * QBLK - klen:2 * QBLK]], axis=1)
            scores.append(_dot_nt(jnp.concatenate([q, eye], axis=1), k_aug))
        for i, (di, r, q0, k0, klen) in enumerate(group):
            s = scores[2 * i:2 * i + 2]
            m = [jnp.max(x, axis=1, keepdims=True) for x in s]
            p = [jnp.exp2(x - mx) for x, mx in zip(s, m)]
            l = [jnp.sum(x, axis=1, keepdims=True) for x in p]
            num = [_dot(x.astype(BF16), vs[di][0, r, k0:k0 + klen, :]) for x in p]
            n_u = jnp.where(lo, num[0], num[1])
            m_u = jnp.where(lo, m[0], m[1])
            l_u = jnp.where(lo, l[0], l[1])
            if di < n_pat - 1:
                d = dils[di]
                f = d_max // d
                for dst, val in ((nd[di], n_u), (md[di], m_u), (ld[di], l_u)):
                    by_res = jnp.swapaxes(val.reshape(QBLK // f, f, LANES), 0, 1)
                    for c in range(f):
                        dst[d * c + r, q0 // f:(q0 + QBLK) // f, :] = by_res[c]
                continue
            ms = [mr[r] for mr in md] + [m_u]
            top = functools.reduce(jnp.maximum, ms)
            w = [jnp.exp2(x - top) for x in ms]
            acc = sum(wx * nx for wx, nx in zip(w, [nr[r] for nr in nd] + [n_u]))
            den = sum(wx * lx for wx, lx in zip(w, [lr[r] for lr in ld] + [l_u]))
            out_scr[r] = acc * (1.0 / den)

    d_max = dils[-1]
    for i in range(0, len(units), ATTN_UNITS_PER_GROUP):
        run(units[i:i + ATTN_UNITS_PER_GROUP])
    rows_per = t_len // d_max // 4
    for blk in range(4):
        x = out_scr[:, blk * rows_per:(blk + 1) * rows_per, :]
        o_ref[0, blk * rows_per * d_max:(blk + 1) * rows_per * d_max, :] = (
            jnp.swapaxes(x, 0, 1).reshape(rows_per * d_max, LANES).astype(o_ref.dtype))


def _prompt_attn_call(streams, t_len):
    bsz = streams[0].shape[0]
    dils = [d for _, d in DILATED_PATTERNS]
    assert all(w // d == QBLK for w, d in DILATED_PATTERNS) and t_len == QBLK * dils[-1] == QBLK * max(dils)
    a = np.arange(QBLK)[:, None]
    c = np.arange(2 * QBLK)[None, :]
    mask_t = jnp.asarray(np.where((c >= a) & (c <= a + QBLK), 0.0, NEG).astype(np.float32).T, dtype=BF16)
    n_grp = ATT_W // LANES
    in_specs, args = [], []
    for s_d in streams:
        d, n = s_d.shape[1], s_d.shape[2]
        for part in range(3):
            in_specs.append(pl.BlockSpec((1, d, n, LANES), lambda b, j, part=part: (b, 0, 0, part * n_grp + j)))
            args.append(s_d)
    by_residue = pltpu.VMEM((dils[-1], t_len // dils[-1], LANES), F32)
    scratch = [by_residue] * (1 + 3 * (len(dils) - 1))
    return pl.pallas_call(
        _prompt_attn_kernel, grid=(bsz, n_grp),
        in_specs=in_specs + [_const_spec((2 * QBLK, QBLK))],
        out_specs=pl.BlockSpec((1, t_len, LANES), lambda b, j: (b, 0, j)),
        out_shape=jax.ShapeDtypeStruct((bsz, t_len, ATT_W), BF16), name="prompt_attn",
        scratch_shapes=scratch,
        compiler_params=_params("parallel", "parallel"),
    )(*args, mask_t)


def _sample_attn_kernel(with_prev, *refs):
    if with_prev:
        q_ref, kn_ref, vn_ref, ck_ref, cv_ref, bias_ref, _, _, o_ref, ok_ref, ov_ref, kb_scr, vb_scr = refs
    else:
        q_ref, kn_ref, vn_ref, ck_ref, cv_ref, bias_ref, o_ref, ok_ref, ov_ref, kb_scr, vb_scr = refs
    win = ck_ref.shape[3]
    t_new = kn_ref.shape[1]
    tail = _iota((LANES, LANES), 1) >= LANES - t_new
    for c_ref, n_ref, out_ref, scr in ((ck_ref, kn_ref, ok_ref, kb_scr), (cv_ref, vn_ref, ov_ref, vb_scr)):
        new_t = jnp.concatenate([n_ref[0], jnp.zeros((LANES - t_new, ATT_W), F32)], axis=0).T
        for rb in range(ATT_W // LANES):
            rows = slice(rb * LANES, (rb + 1) * LANES)
            old = c_ref[0, 0, rows, :]
            shifted = pltpu.roll(old, win - t_new, 1)
            out_ref[0, 0, rows, 0:win - LANES] = shifted[:, 0:win - LANES]
            out_ref[0, 0, rows, win - LANES:win] = jnp.where(
                tail, pltpu.roll(new_t[rows], LANES - t_new, 1), shifted[:, win - LANES:win])
            scr[rows, 0:win] = old.astype(BF16)
            scr[rows, win:win + LANES] = new_t[rows].astype(BF16)
    q = q_ref[0].astype(F32)
    rows = ATT_HEADS * t_new
    own = (_iota((rows, ATT_W), 0) // t_new) == (_iota((rows, ATT_W), 1) // HEAD_DIM)
    q_stack = jnp.where(own, jnp.concatenate([q] * ATT_HEADS, axis=0), 0.0).astype(BF16)
    s = _dot(q_stack, kb_scr[...]) + bias_ref[...]
    m = jnp.max(s, axis=1, keepdims=True)
    p = jnp.exp2(s - m)
    l = jnp.sum(p, axis=1, keepdims=True)
    o_all = jnp.where(own, _dot_nt(p.astype(BF16), vb_scr[...]) * (1.0 / l), 0.0)
    o = o_all[0:t_new]
    for h in range(1, ATT_HEADS):
        o = o + o_all[h * t_new:(h + 1) * t_new]
    o_ref[0] = o.astype(o_ref.dtype)


def _sample_attn_call(q, k_new, v_new, cache_kt, cache_vt, prev, layer):
    depth, bsz, _, win = cache_kt.shape
    t_new = q.shape[1]
    rows = ATT_HEADS * t_new
    tq = np.arange(rows)[:, None] % t_new
    n = np.arange(win + LANES)[None, :]
    bias = np.where(n < win + t_new, _log_multiplicity(win + tq - n), NEG).astype(np.float32)
    new = pl.BlockSpec((1, t_new, ATT_W), lambda b: (b, 0, 0))
    cache = pl.BlockSpec((1, 1, ATT_W, win), lambda b: (layer, b, 0, 0))
    in_specs = [new, new, new, cache, cache, _const_spec((rows, win + LANES))]
    args = [q, k_new, v_new, cache_kt, cache_vt, jnp.asarray(bias)]
    aliases = {}
    if prev is not None:
        in_specs += [pl.BlockSpec(memory_space=pl.ANY)] * 2
        args += list(prev)
        aliases = {6: 1, 7: 2}
    sd = jax.ShapeDtypeStruct
    return pl.pallas_call(
        functools.partial(_sample_attn_kernel, prev is not None), grid=(bsz,),
        in_specs=in_specs, out_specs=[new, cache, cache],
        out_shape=[sd((bsz, t_new, ATT_W), BF16), sd(cache_kt.shape, F32), sd(cache_vt.shape, F32)],
        scratch_shapes=[pltpu.VMEM((ATT_W, win + LANES), BF16)] * 2,
        input_output_aliases=aliases, name="sample_attn", compiler_params=_params("parallel"),
    )(*args)


FF_CHUNK = 1024


def _mlp_kernel(h_ref, oh_ref, oa_ref, og_ref, p_ref, an_ref, wo_ref, npm_ref, npre_ref, wup_ref, wdn_ref,
                npost_ref, wpg_ref, wple_ref, out_ref):
    half = h_ref.shape[0] // 2
    halves = [slice(0, half), slice(half, 2 * half)]
    oa = [(_rms(oa_ref[s, :].astype(F32)) * an_ref[...]).astype(BF16) for s in halves]
    mix = [_dot(jnp.concatenate([oh_ref[s, :], oa[i], og_ref[s, :]], axis=1), wo_ref[...])
           for i, s in enumerate(halves)]
    ple = [_dot(p_ref[s, :].astype(BF16), wple_ref[...]) for s in halves]
    h = [h_ref[s, :] + _rms(mix[i]) * npm_ref[...] for i, s in enumerate(halves)]
    xn = [(_rms(x) * npre_ref[...]).astype(BF16) for x in h]
    acc = [None, None]
    for c in range(D_FF // FF_CHUNK):
        cols = slice(c * FF_CHUNK, (c + 1) * FF_CHUNK)
        u = [jnp.maximum(_dot(x, wup_ref[:, cols]), 0.0) for x in xn]
        part = [_dot((x * x).astype(BF16), wdn_ref[cols, :]) for x in u]
        acc = [p if a is None else a + p for a, p in zip(acc, part)]
    h = [x + _rms(a) * npost_ref[...] for x, a in zip(h, acc)]
    gate = [_sigmoid(_dot(x.astype(BF16), wpg_ref[...])) for x in h]
    for i, s in enumerate(halves):
        out_ref[s, :] = h[i] + gate[i] * ple[i]


def _mlp_call(h2d, oh, oa, og, p3d, layer, an, wo, npm, npre, wup, wdn, npost, wpg, wple, tm):
    n = h2d.shape[0]
    row = lambda w: pl.BlockSpec((tm, w), lambda i: (i, 0))
    lspec = lambda *shape: _layer_spec(shape, layer)
    vec = lspec(1, D_MODEL)
    return pl.pallas_call(
        _mlp_kernel, grid=(n // tm,),
        in_specs=[row(D_MODEL), row(HG_W), row(ATT_W), row(GLA_VW),
                  pl.BlockSpec((None, tm, D_PLE), lambda i: (layer, i, 0)), lspec(1, ATT_W),
                  lspec(D_MODEL, D_MODEL), vec, vec, lspec(D_MODEL, D_FF), lspec(D_FF, D_MODEL), vec,
                  lspec(D_MODEL, D_MODEL), lspec(D_PLE, D_MODEL)],
        out_specs=row(D_MODEL), out_shape=jax.ShapeDtypeStruct((n, D_MODEL), F32), name="mlp",
        compiler_params=_params("parallel"),
    )(h2d, oh, oa, og, p3d, an, wo, npm, npre, wup, wdn, npost, wpg, wple)


def _pack_w_in(w_in):
    splits = np.cumsum([0, HG_W, HG_W, HG_W, HG_W, ATT_W, ATT_W, ATT_W, GLA_KW, GLA_KW, GLA_VW,
                        GLA_GATE_RANK, GLA_VW])
    col = lambda i: w_in[..., splits[i]:splits[i + 1]]
    padc = lambda a, w: jnp.pad(a, ((0, 0), (0, 0), (0, w - a.shape[-1])))
    parts = [col(0), col(1), col(2), col(3), col(4), col(5), col(6),
             padc(col(7), GLA_KP), padc(col(8), GLA_KP), col(9), col(11), padc(col(10), LANES)]
    return jnp.concatenate(parts, axis=-1).astype(BF16)


def _rope_tables(pos):
    half = ROT_DIM // 2
    inv = jnp.exp(-math.log(ROPE_THETA) * jnp.arange(half, dtype=F32) * (2.0 / ROT_DIM))
    ang = pos[:, None] * inv[None, :]
    cos, sin = jnp.cos(ang), jnp.sin(ang)
    d = np.arange(LANES) % HEAD_DIM
    first, second = d < half, (d >= half) & (d < ROT_DIM)
    idx = np.where(second, d - half, np.where(first, d, 0))
    cos_t = jnp.where(first | second, cos[:, idx], 1.0)
    sin_a = jnp.where(first, -sin[:, idx], 0.0)
    sin_b = jnp.where(second, sin[:, idx], 0.0)
    return cos_t, sin_a, sin_b


def kernel(x_prompt, x_sample, state_hgrn, state_gla, cache_k, cache_v, p_prompt, p_sample, norm_pre_mix, w_in, hgrn_lb, hgrn_norm, attn_norm, gla_w_gate2, gla_b_gate, gla_norm, w_out, norm_post_mix, norm_pre_mlp, w_up, w_down, norm_post_mlp, w_ple_gate, w_ple):
    depth = w_in.shape[0]
    bp, tp, _ = x_prompt.shape
    bs, ts, _ = x_sample.shape
    win = cache_k.shape[2]

    lb_cum = jnp.cumsum(jax.nn.softmax(hgrn_lb.astype(F32), axis=0), axis=0)
    lower_bounds = lb_cum - lb_cum[0:1]

    tm_p = min(TOKEN_TILE, tp)
    tm_s = bs * ts
    rope_p = _rope_tables(jnp.arange(tp, dtype=F32))
    rope_s = tuple(jnp.tile(t, (bs, 1)) for t in _rope_tables(jnp.arange(ts, dtype=F32) + PAST_LEN))
    to_t = lambda a: jnp.transpose(a, (0, 1, 3, 4, 2)).reshape(depth, a.shape[1], ATT_W, a.shape[2])
    from_t = lambda a: jnp.transpose(a.reshape(depth, a.shape[1], ATT_HEADS, HEAD_DIM, a.shape[3]), (0, 1, 4, 2, 3))
    ck_t, cv_t = to_t(cache_k), to_t(cache_v)
    pp3 = p_prompt.reshape(depth, bp * tp, D_PLE)
    ps3 = p_sample.reshape(depth, bs * ts, D_PLE)

    hp = x_prompt.reshape(bp * tp, D_MODEL)
    hs = x_sample.reshape(bs * ts, D_MODEL)
    kv_p = None
    kv_s = None
    hg_p, gl_p, hg_s, gl_s = [], [], [], []
    row = lambda a: a.reshape(depth, 1, -1)
    w_all = _pack_w_in(w_in)
    wg2 = jnp.pad(gla_w_gate2, ((0, 0), (0, LANES - GLA_GATE_RANK), (0, GLA_KP - GLA_KW))).astype(BF16)
    bg = row(jnp.pad(gla_b_gate, ((0, 0), (0, GLA_KP - GLA_KW))))
    proj_w = (row(norm_pre_mix), w_all, row(lower_bounds))
    mlp_w = (row(attn_norm), w_out.astype(BF16), row(norm_post_mix), row(norm_pre_mlp), w_up.astype(BF16),
             w_down.astype(BF16), row(norm_post_mlp), w_ple_gate.astype(BF16), w_ple.astype(BF16))
    hgrn_nw, gla_nw = row(hgrn_norm), row(gla_norm)
    for i in range(depth):
        (hq, hk, hv, hlf, hgt, gq, gk, gv, gla, ggt, *att_streams, k_t, v_t) = _proj_call(
            hp, *proj_w, rope_p, wg2, bg, kv_p, i, i, depth, tm_p, tp)
        kv_p = (k_t, v_t)
        sh = lambda a: a.reshape(bp, tp, a.shape[-1])
        (oh, s_h), (og, s_g) = _recur_call(
            [(sh(hq), sh(hk), sh(hv), sh(hlf), sh(hgt), hgrn_nw, None, HG_HEADS, HG_DK, HG_DV),
             (sh(gq), sh(gk), sh(gv), sh(gla), sh(ggt), gla_nw, None, GLA_HEADS, GLA_DK, GLA_DV)], i, "recur")
        oa = _prompt_attn_call(att_streams, tp)
        fl = lambda a: a.reshape(bp * tp, a.shape[-1])
        hp = _mlp_call(hp, fl(oh), fl(oa), fl(og), pp3, i, *mlp_w, tm_p)
        hg_p.append(s_h.reshape(bp, HG_HEADS, HG_DK, HG_DV))
        gl_p.append(s_g.reshape(bp, GLA_HEADS, GLA_DK, GLA_DV))

        (hq, hk, hv, hlf, hgt, gq, gk, gv, gla, ggt, aq, k_new, v_new) = _proj_call(
            hs, *proj_w, rope_s, wg2, bg, None, i, 0, 1, tm_s, None)
        pt = lambda a: a.reshape(bs, ts, a.shape[-1])
        (oh, s_h), (og, s_g) = _recur_call(
            [(pt(hq), pt(hk), pt(hv), pt(hlf), pt(hgt), hgrn_nw, state_hgrn[i].reshape(bs, HG_W, HG_DV),
              HG_HEADS, HG_DK, HG_DV),
             (pt(gq), pt(gk), pt(gv), pt(gla), pt(ggt), gla_nw, state_gla[i].reshape(bs, GLA_KW, GLA_DV),
              GLA_HEADS, GLA_DK, GLA_DV)], i, "recur_s")
        s3 = lambda a: a.reshape(bs, ts, ATT_W)
        oa, ck_new, cv_new = _sample_attn_call(s3(aq), s3(k_new), s3(v_new), ck_t, cv_t, kv_s, i)
        kv_s = (ck_new, cv_new)
        ut = lambda a: a.reshape(bs * ts, a.shape[-1])
        hs = _mlp_call(hs, ut(oh), oa.reshape(bs * ts, ATT_W), ut(og), ps3, i, *mlp_w, tm_s)
        hg_s.append(s_h.reshape(bs, HG_HEADS, HG_DK, HG_DV))
        gl_s.append(s_g.reshape(bs, GLA_HEADS, GLA_DK, GLA_DV))

    return (hp.reshape(bp, tp, D_MODEL), hs.reshape(bs, ts, D_MODEL),
            jnp.stack(hg_p), jnp.stack(gl_p), from_t(kv_p[0]), from_t(kv_p[1]),
            jnp.stack(hg_s), jnp.stack(gl_s), from_t(kv_s[0]), from_t(kv_s[1]))
```
